```python
import math
import jax, jax.numpy as jnp
from jax import lax
import numpy as np

D_MODEL = 2048
BATCH = 4
SEQ = 2048
DEPTH = 4
DEC_BATCH = 128
DEC_SEQ = 4
PAST_LEN = 16384
PAGE_SIZE = 128

N_MIXERS = 2
N_DELTA = (DEPTH + 1) // 2
N_POOL = DEPTH // 2
HEAD_DIM = 128
MIX_W = D_MODEL
N_MEM = 256
N_MEM_HEADS = 4
MEM_W = N_MEM_HEADS * HEAD_DIM
TOK_W = MIX_W - MEM_W
N_DELTA_HEADS = TOK_W // HEAD_DIM
QKV_W = 3 * TOK_W
CONV_W = 4
CHUNK = 64
POOL_WINDOWS = (2, 4, 8, 16)
N_POOL_GROUPS = len(POOL_WINDOWS)
POOL_GROUP_W = TOK_W // N_POOL_GROUPS
POOL_BUF = max(POOL_WINDOWS) - 1
D_FF = ((8 * D_MODEL + 3 * 256 - 1) // (3 * 256)) * 256
IN_DELTA = QKV_W + TOK_W + 2 * N_DELTA_HEADS + MEM_W
IN_POOL = TOK_W + MEM_W
EPS = 1e-6

kernel_name = 'hybrid_gdn_pool_memxattn_step'


def rmsnorm(x, w):
    xf = x.astype(jnp.float32)
    y = xf * lax.rsqrt(jnp.mean(xf * xf, axis=-1, keepdims=True) + EPS)
    return (y * w.astype(jnp.float32)).astype(x.dtype)


def l2norm(x):
    xf = x.astype(jnp.float32)
    return xf * lax.rsqrt(jnp.sum(xf * xf, axis=-1, keepdims=True) + EPS)


def causal_conv(x, buf, w):
    xp = jnp.concatenate([buf.astype(x.dtype), x], axis=1)
    y = lax.conv_general_dilated(xp, w[:, None, :].astype(x.dtype), window_strides=(1,), padding='VALID',
                                 dimension_numbers=('NWC', 'WIO', 'NWC'), feature_group_count=x.shape[-1])
    return jax.nn.silu(y), xp[:, -(CONV_W - 1):]


def gated_delta_chunked(q, k, v, g, beta, S0):
    B, L, H, _ = q.shape
    DV = v.shape[-1]
    c = min(CHUNK, L)
    pad = (-L) % c
    if pad:
        padf = lambda a: jnp.pad(a, [(0, 0), (0, pad)] + [(0, 0)] * (a.ndim - 2))
        q, k, v, g, beta = (padf(a) for a in (q, k, v, g, beta))
    n = (L + pad) // c
    q, k, v, g, beta = (jnp.moveaxis(a.reshape(B, n, c, H, *a.shape[3:]), 3, 1) for a in (q, k, v, g, beta))
    gc = jnp.cumsum(g, axis=-1)
    diff = gc[..., :, None] - gc[..., None, :]
    idx = jnp.arange(c)
    incl = idx[:, None] >= idx[None, :]
    strict = idx[:, None] > idx[None, :]
    decay = jnp.exp(jnp.where(incl, diff, -jnp.inf))
    kk = jnp.einsum('bhnik,bhnjk->bhnij', k, k)
    lmat = jnp.where(strict, kk * decay, 0.0) * beta[..., :, None]
    eye = jnp.eye(c, dtype=jnp.float32)
    gam = jnp.exp(gc)
    rhs = jnp.concatenate([v * beta[..., None], k * (beta * gam)[..., None]], axis=-1)
    sol = lax.linalg.triangular_solve(eye + lmat, rhs, left_side=True, lower=True, unit_diagonal=True)
    u_v, w_k = sol[..., :DV], sol[..., DV:]
    attn = jnp.einsum('bhnik,bhnjk->bhnij', q, k) * decay
    q_g = q * gam[..., None]
    k_end = k * jnp.exp(gc[..., -1:] - gc)[..., None]
    g_end = jnp.exp(gc[..., -1])

    def step(S, xs):
        qg_c, attn_c, uv_c, wk_c, kend_c, gend_c = xs
        u = uv_c - jnp.einsum('bhck,bhkv->bhcv', wk_c, S)
        o = jnp.einsum('bhck,bhkv->bhcv', qg_c, S) + jnp.einsum('bhij,bhjv->bhiv', attn_c, u)
        S = S * gend_c[..., None, None] + jnp.einsum('bhck,bhcv->bhkv', kend_c, u)
        return S, o

    xs = tuple(jnp.moveaxis(a, 2, 0) for a in (q_g, attn, u_v, w_k, k_end, g_end))
    S, o = lax.scan(step, S0, xs)
    o = jnp.moveaxis(o, 0, 2).reshape(B, H, n * c, DV)[:, :, :L]
    return jnp.moveaxis(o, 1, 2), S


def delta_mixer(h, conv_buf, S0, w_in, conv_w, a_log, dt_bias, onorm_w):
    B, L, _ = h.shape
    H = N_DELTA_HEADS
    proj = h @ w_in
    qkv, z, a, b, q_mem = jnp.split(proj, [QKV_W, QKV_W + TOK_W, QKV_W + TOK_W + H, QKV_W + TOK_W + 2 * H], axis=-1)
    qkv, conv_new = causal_conv(qkv, conv_buf, conv_w)
    q, k, v = (t.reshape(B, L, H, HEAD_DIM) for t in jnp.split(qkv, 3, axis=-1))
    q = l2norm(q) * HEAD_DIM ** -0.5
    k = l2norm(k)
    v = v.astype(jnp.float32)
    beta = jax.nn.sigmoid(b.astype(jnp.float32))
    g = -jnp.exp(a_log.astype(jnp.float32)) * jax.nn.softplus(a.astype(jnp.float32) + dt_bias.astype(jnp.float32))
    o, S = gated_delta_chunked(q, k, v, g, beta, S0.astype(jnp.float32))
    zf = z.reshape(B, L, H, HEAD_DIM).astype(jnp.float32)
    o = rmsnorm(o, onorm_w) * jax.nn.silu(zf)
    return o.reshape(B, L, TOK_W).astype(h.dtype), q_mem, conv_new, S.astype(S0.dtype)


def pool_mixer(h, buf, n_past, w_in, w_grp, scale):
    B, L, _ = h.shape
    proj = h @ w_in
    u, q_mem = proj[..., :TOK_W], proj[..., TOK_W:]
    up = jnp.concatenate([buf.astype(u.dtype), u], axis=1).astype(jnp.float32)
    csum = jnp.pad(jnp.cumsum(up, axis=1), ((0, 0), (1, 0), (0, 0)))
    end = csum[:, POOL_BUF + 1:]
    t = jnp.arange(L)
    outs = []
    for gi, w in enumerate(POOL_WINDOWS):
        sl = slice(gi * POOL_GROUP_W, (gi + 1) * POOL_GROUP_W)
        start = csum[:, POOL_BUF + 1 - w: POOL_BUF + 1 - w + L, sl]
        cnt = jnp.minimum(w, t + 1 + n_past).astype(jnp.float32)
        outs.append((end[..., sl] - start) / cnt[None, :, None])
    d = jnp.concatenate(outs, axis=-1) - up[:, POOL_BUF:]
    d = d.reshape(B, L, N_POOL_GROUPS, POOL_GROUP_W)
    y = jnp.einsum('blgc,gcd->blgd', d, w_grp.astype(jnp.float32)).reshape(B, L, TOK_W) * scale.astype(jnp.float32)
    return y.astype(h.dtype), q_mem, up[:, -POOL_BUF:].astype(buf.dtype)


def mem_kv(mem, norm_w, w_kv):
    B, M, _ = mem.shape
    k, v = jnp.split(rmsnorm(mem, norm_w) @ w_kv, 2, axis=-1)
    return k.reshape(B, M, N_MEM_HEADS, HEAD_DIM), v.reshape(B, M, N_MEM_HEADS, HEAD_DIM)


def cross_attn(q_mem, mk, mv):
    B, L, _ = q_mem.shape
    q = q_mem.reshape(B, L, N_MEM_HEADS, HEAD_DIM)
    s = jnp.einsum('blhd,bmhd->bhlm', q, mk.astype(q.dtype)).astype(jnp.float32) * HEAD_DIM ** -0.5
    p = jax.nn.softmax(s, axis=-1).astype(q.dtype)
    return jnp.einsum('bhlm,bmhd->blhd', p, mv.astype(q.dtype)).reshape(B, L, MEM_W)


def swiglu(h, w_gate_up, w_down):
    gate, up = jnp.split(h @ w_gate_up, 2, axis=-1)
    return (jax.nn.silu(gate) * up) @ w_down


def trunk(x, mem_k, mem_v, S_in, conv_in, pool_in, n_past, norm_mix, norm_ffn, norm_final,
          w_in_delta, conv_w, a_log, dt_bias, delta_onorm, w_in_pool, w_pool_grp, pool_scale,
          w_out, w_gate_up, w_down):
    S_out, conv_out, pool_out = [], [], []
    di = pi = 0
    for l in range(DEPTH):
        h = rmsnorm(x, norm_mix[l])
        if l % N_MIXERS == 0:
            y_tok, q_mem, conv_new, S_new = delta_mixer(h, conv_in[di], S_in[di], w_in_delta[di], conv_w[di],
                                                        a_log[di], dt_bias[di], delta_onorm[di])
            S_out.append(S_new)
            conv_out.append(conv_new)
            di += 1
        else:
            y_tok, q_mem, buf_new = pool_mixer(h, pool_in[pi], n_past, w_in_pool[pi], w_pool_grp[pi], pool_scale[pi])
            pool_out.append(buf_new)
            pi += 1
        y_mem = cross_attn(q_mem, mem_k[l], mem_v[l])
        x = x + jnp.concatenate([y_tok, y_mem], axis=-1) @ w_out[l]
        x = x + swiglu(rmsnorm(x, norm_ffn[l]), w_gate_up[l], w_down[l])
    return rmsnorm(x, norm_final), jnp.stack(S_out), jnp.stack(conv_out), jnp.stack(pool_out)


def setup_inputs(seed: int = 0) -> dict:
    key = jax.random.key(seed)
    ks = iter(jax.random.split(key, 40))
    nrm = lambda shape, scale: jax.random.normal(next(ks), shape, jnp.float32) * scale
    gain = lambda shape: 1.0 + 0.02 * jax.random.normal(next(ks), shape, jnp.float32)
    a_log = jnp.log(jax.random.uniform(next(ks), (N_DELTA, N_DELTA_HEADS), jnp.float32, 1.0, 16.0))
    dt = jnp.exp(jax.random.uniform(next(ks), (N_DELTA, N_DELTA_HEADS), jnp.float32, math.log(1e-3), math.log(1e-1)))
    dt_bias = dt + jnp.log(-jnp.expm1(-dt))
    return {
        'x_prompt': nrm((BATCH, SEQ, D_MODEL), 1.0),
        'x_sample': nrm((DEC_BATCH, DEC_SEQ, D_MODEL), 1.0),
        'mem_prompt': nrm((BATCH, N_MEM, D_MODEL), 1.0),
        'state_delta_S': nrm((N_DELTA, DEC_BATCH, N_DELTA_HEADS, HEAD_DIM, HEAD_DIM), HEAD_DIM ** -0.5),
        'state_delta_conv': nrm((N_DELTA, DEC_BATCH, CONV_W - 1, QKV_W), 1.0),
        'state_pool': nrm((N_POOL, DEC_BATCH, POOL_BUF, TOK_W), 1.0),
        'cache_mem_k': nrm((DEPTH, DEC_BATCH, N_MEM, N_MEM_HEADS, HEAD_DIM), 1.0),
        'cache_mem_v': nrm((DEPTH, DEC_BATCH, N_MEM, N_MEM_HEADS, HEAD_DIM), 1.0),
        'norm_mix': gain((DEPTH, D_MODEL)),
        'norm_ffn': gain((DEPTH, D_MODEL)),
        'norm_mem': gain((DEPTH, D_MODEL)),
        'norm_final': gain((D_MODEL,)),
        'w_in_delta': nrm((N_DELTA, D_MODEL, IN_DELTA), D_MODEL ** -0.5),
        'conv_w': nrm((N_DELTA, CONV_W, QKV_W), CONV_W ** -0.5),
        'a_log': a_log,
        'dt_bias': dt_bias,
        'delta_onorm': gain((N_DELTA, HEAD_DIM)),
        'w_in_pool': nrm((N_POOL, D_MODEL, IN_POOL), D_MODEL ** -0.5),
        'w_pool_grp': nrm((N_POOL, N_POOL_GROUPS, POOL_GROUP_W, POOL_GROUP_W), POOL_GROUP_W ** -0.5),
        'pool_scale': 1.0 + 0.1 * jax.random.normal(next(ks), (N_POOL, TOK_W), jnp.float32),
        'w_mem_kv': nrm((DEPTH, D_MODEL, 2 * MEM_W), D_MODEL ** -0.5),
        'w_out': nrm((DEPTH, MIX_W, D_MODEL), MIX_W ** -0.5),
        'w_gate_up': nrm((DEPTH, D_MODEL, 2 * D_FF), D_MODEL ** -0.5),
        'w_down': nrm((DEPTH, D_FF, D_MODEL), D_FF ** -0.5),
    }


def reference(x_prompt, x_sample, mem_prompt, state_delta_S, state_delta_conv, state_pool, cache_mem_k, cache_mem_v,
              norm_mix, norm_ffn, norm_mem, norm_final, w_in_delta, conv_w, a_log, dt_bias, delta_onorm,
              w_in_pool, w_pool_grp, pool_scale, w_mem_kv, w_out, w_gate_up, w_down):
    weights = (norm_mix, norm_ffn, norm_final, w_in_delta, conv_w, a_log, dt_bias, delta_onorm,
               w_in_pool, w_pool_grp, pool_scale, w_out, w_gate_up, w_down)
    B = x_prompt.shape[0]
    mkv = [mem_kv(mem_prompt, norm_mem[l], w_mem_kv[l]) for l in range(DEPTH)]
    p_mem_k = jnp.stack([kv[0] for kv in mkv])
    p_mem_v = jnp.stack([kv[1] for kv in mkv])
    S0 = jnp.zeros((N_DELTA, B, N_DELTA_HEADS, HEAD_DIM, HEAD_DIM), x_prompt.dtype)
    conv0 = jnp.zeros((N_DELTA, B, CONV_W - 1, QKV_W), x_prompt.dtype)
    pool0 = jnp.zeros((N_POOL, B, POOL_BUF, TOK_W), x_prompt.dtype)
    y_prompt, p_delta_S, p_delta_conv, p_pool = trunk(x_prompt, p_mem_k, p_mem_v, S0, conv0, pool0, 0, *weights)
    n_past = min(PAST_LEN, POOL_BUF)
    y_sample, s_delta_S, s_delta_conv, s_pool = trunk(x_sample, cache_mem_k, cache_mem_v, state_delta_S,
                                                      state_delta_conv, state_pool, n_past, *weights)
    return (y_prompt, y_sample, p_delta_S, p_delta_conv, p_pool, p_mem_k, p_mem_v, s_delta_S, s_delta_conv, s_pool)
```

```python
import functools
import math

import jax
import jax.numpy as jnp
from jax import lax
from jax.experimental import pallas as pl
from jax.experimental.pallas import tpu as pltpu

F32 = jnp.float32
BF16 = jnp.bfloat16
HIGHEST = lax.Precision.HIGHEST

EPS = 1e-6
HEAD_DIM = 128
N_MEM_HEADS = 4
MEM_W = N_MEM_HEADS * HEAD_DIM
CONV_W = 4
POOL_WINDOWS = (2, 4, 8, 16)
POOL_HIST = max(POOL_WINDOWS)
LANE = 128
SUBLANE = 8
CARRY_OFF = SUBLANE - (CONV_W - 1)
VMEM_LIMIT = 56 * 1024 * 1024


def _params(*sem):
    return pltpu.CompilerParams(dimension_semantics=sem, vmem_limit_bytes=VMEM_LIMIT)


def _sigmoid(x):
    return 1.0 / (1.0 + jnp.exp(-x))


def _dot(a, b, precision=None):
    return jnp.dot(a, b, preferred_element_type=F32, precision=precision)


def _dot_nt(a, b, precision=None):
    return lax.dot_general(a, b, (((1,), (1,)), ((), ())), preferred_element_type=F32, precision=precision)


def _dot_tn(a, b, precision=None):
    return lax.dot_general(a, b, (((0,), (0,)), ((), ())), preferred_element_type=F32, precision=precision)


def _bdot(a, b):
    return _dot(a.astype(BF16), b.astype(BF16))


def _bdot_nt(a, b):
    return _dot_nt(a.astype(BF16), b.astype(BF16))


def _bdot_tn(a, b):
    return _dot_tn(a.astype(BF16), b.astype(BF16))


def _rmsnorm_body(x_ref, w_ref, o_ref):
    x = x_ref[...]
    ms = jnp.mean(x * x, axis=-1, keepdims=True)
    o_ref[...] = (x * lax.rsqrt(ms + EPS) * w_ref[...]).astype(o_ref.dtype)


def rmsnorm(x, w, out_dtype, tm=512):
    t, d = x.shape
    return pl.pallas_call(
        _rmsnorm_body,
        grid=(t // tm,),
        in_specs=[pl.BlockSpec((tm, d), lambda i: (i, 0)), pl.BlockSpec((1, d), lambda i: (0, 0))],
        out_specs=pl.BlockSpec((tm, d), lambda i: (i, 0)),
        out_shape=jax.ShapeDtypeStruct((t, d), out_dtype),
        compiler_params=_params("arbitrary"),
        name="rmsnorm",
    )(x, w.reshape(1, d))


def _mm_body(*refs, n_w, tn, has_res, swiglu):
    x_ref = refs[0]
    w_refs = refs[1:1 + n_w]
    res_ref = refs[1 + n_w] if has_res else None
    o_ref = refs[-2]
    wb_ref = refs[-1]

    @pl.when(pl.program_id(1) == 0)
    def _cast_weights():
        for i, w_ref in enumerate(w_refs):
            wb_ref[:, i * tn:(i + 1) * tn] = w_ref[...].astype(BF16)

    acc = jnp.dot(x_ref[...], wb_ref[...], preferred_element_type=F32)
    if swiglu:
        gate, up = acc[:, :tn], acc[:, tn:]
        acc = gate * _sigmoid(gate) * up
    if has_res:
        acc = acc + res_ref[...]
    o_ref[...] = acc.astype(o_ref.dtype)


def matmul(x, w, layer, *, ncols, tn, out_dtype, col_blocks=(0,), residual=None, swiglu=False, tm=512):
    t, k = x.shape
    n_w = len(col_blocks)
    in_specs = [pl.BlockSpec((tm, k), lambda j, m: (m, 0))]
    for cb in col_blocks:
        in_specs.append(pl.BlockSpec((None, k, tn), lambda j, m, cb=cb: (layer, 0, cb + j)))
    args = [x] + [w] * n_w
    if residual is not None:
        in_specs.append(pl.BlockSpec((tm, tn), lambda j, m: (m, j)))
        args.append(residual)
    return pl.pallas_call(
        functools.partial(_mm_body, n_w=n_w, tn=tn, has_res=residual is not None, swiglu=swiglu),
        grid=(ncols // tn, t // tm),
        in_specs=in_specs,
        out_specs=pl.BlockSpec((tm, tn), lambda j, m: (m, j)),
        out_shape=jax.ShapeDtypeStruct((t, ncols), out_dtype),
        scratch_shapes=[pltpu.VMEM((k, n_w * tn), BF16)],
        compiler_params=_params("arbitrary", "arbitrary"),
        name="matmul",
    )(*args)


def _delta_body(main_ref, small_ref, cbuf_ref, s0_ref, cw_ref, alog_ref, dtb_ref, onw_ref,
                y_ref, cnew_ref, s_ref, xp_scr, ab_scr, *, c_real, c_pad, n_heads, tok_w):
    qkv_w = 3 * tok_w
    c = c_pad
    padded = c_pad > c_real

    @pl.when(pl.program_id(1) == 0)
    def _init():
        s_ref[...] = s0_ref[...]
        xp_scr[CARRY_OFF:SUBLANE, 0:qkv_w] = cbuf_ref[...]
        if padded:
            xp_scr[SUBLANE + c_real:SUBLANE + c, :] = jnp.zeros((c - c_real, xp_scr.shape[1]), F32)
            ab_scr[...] = jnp.zeros(ab_scr.shape, F32)

    xp_scr[SUBLANE:SUBLANE + c_real, :] = main_ref[...]
    ab_scr[0:c_real, :] = small_ref[:, MEM_W:MEM_W + LANE]

    ab = ab_scr[...]
    a_in = ab + dtb_ref[...]
    softplus = jnp.maximum(a_in, 0.0) + jnp.log(1.0 + jnp.exp(-jnp.abs(a_in)))
    g_all = -jnp.exp(alog_ref[...]) * softplus
    beta_all = _sigmoid(ab)
    if padded:
        live = lax.broadcasted_iota(jnp.int32, (c, LANE), 0) < c_real
        g_all = jnp.where(live, g_all, 0.0)
        beta_all = jnp.where(live, beta_all, 0.0)

    rows = lax.broadcasted_iota(jnp.int32, (c, c), 0)
    cols = lax.broadcasted_iota(jnp.int32, (c, c), 1)
    incl = rows >= cols
    strict = rows > cols
    eye = (rows == cols).astype(F32)
    eye_l = (lax.broadcasted_iota(jnp.int32, (LANE, LANE), 0)
             == lax.broadcasted_iota(jnp.int32, (LANE, LANE), 1)).astype(F32)

    gc_all = _dot(incl.astype(F32), g_all, HIGHEST)
    gc_t = _dot_nt(eye_l, gc_all, HIGHEST)
    gam_all = jnp.exp(gc_all)
    g_last = gc_all[c - 1:c, :]
    kdec_all = jnp.exp(g_last - gc_all)
    gend_all = jnp.exp(g_last)

    def conv_slab(col):
        acc = xp_scr[CARRY_OFF:CARRY_OFF + c, col:col + LANE] * cw_ref[0:1, col:col + LANE]
        for i in range(1, CONV_W):
            acc = acc + xp_scr[CARRY_OFF + i:CARRY_OFF + i + c, col:col + LANE] * cw_ref[i:i + 1, col:col + LANE]
        return acc * _sigmoid(acc)

    n_levels = int(math.log2(c))
    for h in range(n_heads):
        lo = h * HEAD_DIM
        q = conv_slab(lo)
        k = conv_slab(tok_w + lo)
        v = conv_slab(2 * tok_w + lo)
        q = q * lax.rsqrt(jnp.sum(q * q, axis=-1, keepdims=True) + EPS) * (HEAD_DIM ** -0.5)
        k = k * lax.rsqrt(jnp.sum(k * k, axis=-1, keepdims=True) + EPS)

        beta = beta_all[:, n_heads + h:n_heads + h + 1]
        gam = gam_all[:, h:h + 1]
        kdec = kdec_all[:, h:h + 1]
        gend = gend_all[:, h:h + 1]
        diff = gc_all[:, h:h + 1] - gc_t[h:h + 1, :]
        decay = jnp.where(incl, jnp.exp(jnp.where(incl, diff, 0.0)), 0.0)

        qk_kk = _bdot_nt(jnp.concatenate([q, k], axis=0), k)
        qk, kk = qk_kk[:c], qk_kk[c:]

        x_pow = -(jnp.where(strict, kk * decay, 0.0) * beta)
        t_inv = eye + x_pow
        for lvl in range(1, n_levels):
            if lvl == 1:
                x_pow = _dot(x_pow, x_pow, HIGHEST)
            else:
                both = _dot(jnp.concatenate([t_inv, x_pow], axis=0), x_pow, HIGHEST)
                t_inv = t_inv + both[:c]
                x_pow = both[c:]
        t_inv = t_inv + _dot(t_inv, x_pow, HIGHEST)

        rhs = jnp.concatenate([v * beta, k * (beta * gam)], axis=1)
        sol = _dot(t_inv, rhs, HIGHEST)
        u_v, w_k = sol[:, :HEAD_DIM], sol[:, HEAD_DIM:]

        s_old = s_ref[h]
        from_state = _bdot(jnp.concatenate([w_k, q * gam], axis=0), s_old)
        u = u_v - from_state[:c]
        o = from_state[c:] + _bdot(jnp.where(incl, qk * decay, 0.0), u)
        s_ref[h] = s_old * gend + _bdot_tn(k * kdec, u)

        o = o * lax.rsqrt(jnp.mean(o * o, axis=-1, keepdims=True) + EPS) * onw_ref[...]
        z = xp_scr[SUBLANE:SUBLANE + c, qkv_w + lo:qkv_w + lo + HEAD_DIM]
        out = o * (z * _sigmoid(z))
        y_ref[:, lo:lo + HEAD_DIM] = out[:c_real].astype(y_ref.dtype)

    carry = xp_scr[CARRY_OFF + c_real:SUBLANE + c_real, 0:qkv_w]
    xp_scr[CARRY_OFF:SUBLANE, 0:qkv_w] = carry
    cnew_ref[...] = carry


def delta_mixer(main, small, conv_buf, s0, s_layer, conv_w, a_log, dt_bias, onorm_w, *, batch, seq, chunk):
    n_heads = s0.shape[2]
    tok_w = n_heads * HEAD_DIM
    main_w = 4 * tok_w
    c_real = min(chunk, seq)
    c_pad = max(c_real, SUBLANE)
    nch = seq // c_real
    if main.ndim == 2:
        row_spec = lambda w: pl.BlockSpec((c_real, w), lambda b, ch: (b * nch + ch, 0))
        y_shape = (batch * seq, tok_w)
    else:
        row_spec = lambda w: pl.BlockSpec((None, c_real, w), lambda b, ch: (b, ch, 0))
        y_shape = (batch, seq, tok_w)
    pad_lane = lambda p: jnp.zeros((1, LANE), F32).at[0, :n_heads].set(p.astype(F32))
    body = functools.partial(_delta_body, c_real=c_real, c_pad=c_pad, n_heads=n_heads, tok_w=tok_w)
    return pl.pallas_call(
        body,
        grid=(batch, nch),
        in_specs=[
            row_spec(main_w),
            row_spec(small.shape[-1]),
            pl.BlockSpec((None, CONV_W - 1, 3 * tok_w), lambda b, ch: (b, 0, 0)),
            pl.BlockSpec((None, None, n_heads, HEAD_DIM, HEAD_DIM), lambda b, ch: (s_layer, b, 0, 0, 0)),
            pl.BlockSpec((CONV_W, 3 * tok_w), lambda b, ch: (0, 0)),
            pl.BlockSpec((1, LANE), lambda b, ch: (0, 0)),
            pl.BlockSpec((1, LANE), lambda b, ch: (0, 0)),
            pl.BlockSpec((1, HEAD_DIM), lambda b, ch: (0, 0)),
        ],
        out_specs=[
            row_spec(tok_w),
            pl.BlockSpec((None, CONV_W - 1, 3 * tok_w), lambda b, ch: (b, 0, 0)),
            pl.BlockSpec((None, n_heads, HEAD_DIM, HEAD_DIM), lambda b, ch: (b, 0, 0, 0)),
        ],
        out_shape=[
            jax.ShapeDtypeStruct(y_shape, F32),
            jax.ShapeDtypeStruct((batch, CONV_W - 1, 3 * tok_w), F32),
            jax.ShapeDtypeStruct((batch, n_heads, HEAD_DIM, HEAD_DIM), F32),
        ],
        scratch_shapes=[pltpu.VMEM((SUBLANE + c_pad, main_w), F32), pltpu.VMEM((c_pad, LANE), F32)],
        compiler_params=_params("arbitrary", "arbitrary"),
        name="delta_mixer",
    )(main, small, conv_buf, s0, conv_w, pad_lane(a_log), pad_lane(dt_bias), onorm_w.reshape(1, HEAD_DIM))


def _pool_body(u_ref, buf_ref, wg_ref, scale_ref, y_ref, pnew_ref, hist_scr, *, c_real, c_pad, n_past, tok_w):
    c = c_pad
    gw = tok_w // len(POOL_WINDOWS)

    @pl.when(pl.program_id(1) == 0)
    def _init():
        hist_scr[0:1, :] = jnp.zeros((1, tok_w), F32)
        hist_scr[1:POOL_HIST, :] = buf_ref[...]
        if c_pad > c_real:
            hist_scr[POOL_HIST + c_real:POOL_HIST + c, :] = jnp.zeros((c - c_real, tok_w), F32)

    hist_scr[POOL_HIST:POOL_HIST + c_real, :] = u_ref[...]
    pos = pl.program_id(1) * c_real + lax.broadcasted_iota(jnp.int32, (c, 1), 0)
    for gi, win in enumerate(POOL_WINDOWS):
        lo = gi * gw
        tok = hist_scr[POOL_HIST:POOL_HIST + c, lo:lo + gw]
        acc = tok
        for s in range(1, win):
            acc = acc + hist_scr[POOL_HIST - s:POOL_HIST - s + c, lo:lo + gw]
        cnt = jnp.minimum(win, pos + 1 + n_past).astype(F32)
        d = acc / cnt - tok
        y = _bdot(d, wg_ref[gi]) * scale_ref[:, lo:lo + gw]
        y_ref[:, lo:lo + gw] = y[:c_real].astype(y_ref.dtype)

    tail = hist_scr[c_real + 1:c_real + POOL_HIST, :]
    hist_scr[1:POOL_HIST, :] = tail
    pnew_ref[...] = tail


def pool_mixer(u, buf, w_grp, scale, layer, *, batch, seq, chunk, n_past):
    tok_w = buf.shape[-1]
    gw = tok_w // len(POOL_WINDOWS)
    c_real = min(chunk, seq)
    c_pad = max(c_real, SUBLANE)
    nch = seq // c_real
    if u.ndim == 2:
        row_spec = pl.BlockSpec((c_real, tok_w), lambda b, ch: (b * nch + ch, 0))
        y_shape = (batch * seq, tok_w)
    else:
        row_spec = pl.BlockSpec((None, c_real, tok_w), lambda b, ch: (b, ch, 0))
        y_shape = (batch, seq, tok_w)
    body = functools.partial(_pool_body, c_real=c_real, c_pad=c_pad, n_past=n_past, tok_w=tok_w)
    return pl.pallas_call(
        body,
        grid=(batch, nch),
        in_specs=[
            row_spec,
            pl.BlockSpec((None, POOL_HIST - 1, tok_w), lambda b, ch: (b, 0, 0)),
            pl.BlockSpec((None, len(POOL_WINDOWS), gw, gw), lambda b, ch: (layer, 0, 0, 0)),
            pl.BlockSpec((None, 1, tok_w), lambda b, ch: (layer, 0, 0)),
        ],
        out_specs=[row_spec, pl.BlockSpec((None, POOL_HIST - 1, tok_w), lambda b, ch: (b, 0, 0))],
        out_shape=[jax.ShapeDtypeStruct(y_shape, F32), jax.ShapeDtypeStruct((batch, POOL_HIST - 1, tok_w), F32)],
        scratch_shapes=[pltpu.VMEM((POOL_HIST + c_pad, tok_w), F32)],
        compiler_params=_params("arbitrary", "arbitrary"),
        name="pool_mixer",
    )(u, buf, w_grp, scale.reshape(scale.shape[0], 1, tok_w))


def _xattn_body(q_ref, k_ref, v_ref, o_ref, q_scr, *, c_real, c_pad):
    if c_pad > c_real:
        q_scr[...] = jnp.zeros(q_scr.shape, F32)
        q_scr[0:c_real, :] = q_ref[...]
        q_all = q_scr[...]
    else:
        q_all = q_ref[...]
    for h in range(N_MEM_HEADS):
        lo = h * HEAD_DIM
        s = _bdot_nt(q_all[:, lo:lo + HEAD_DIM], k_ref[:, lo:lo + HEAD_DIM]) * (HEAD_DIM ** -0.5)
        p = jnp.exp(s - jnp.max(s, axis=-1, keepdims=True))
        p = p / jnp.sum(p, axis=-1, keepdims=True)
        o = _bdot(p, v_ref[:, lo:lo + HEAD_DIM])
        o_ref[:, lo:lo + HEAD_DIM] = o[:c_real].astype(o_ref.dtype)


def cross_attn(q, q_col_block, mk, mv, kv_index, *, batch, seq, tq):
    c_real = min(tq, seq)
    c_pad = max(c_real, SUBLANE)
    nq = seq // c_real
    if q.ndim == 2:
        q_spec = pl.BlockSpec((c_real, MEM_W), lambda b, i: (b * nq + i, q_col_block))
        o_spec = pl.BlockSpec((c_real, MEM_W), lambda b, i: (b * nq + i, 0))
        o_shape = (batch * seq, MEM_W)
    else:
        q_spec = pl.BlockSpec((None, c_real, MEM_W), lambda b, i: (b, i, q_col_block))
        o_spec = pl.BlockSpec((None, c_real, MEM_W), lambda b, i: (b, i, 0))
        o_shape = (batch, seq, MEM_W)
    k_block, k_map = kv_index[0]
    v_block, v_map = kv_index[1]
    return pl.pallas_call(
        functools.partial(_xattn_body, c_real=c_real, c_pad=c_pad),
        grid=(batch, nq),
        in_specs=[q_spec, pl.BlockSpec(k_block, k_map), pl.BlockSpec(v_block, v_map)],
        out_specs=o_spec,
        out_shape=jax.ShapeDtypeStruct(o_shape, F32),
        scratch_shapes=[pltpu.VMEM((c_pad, MEM_W), F32)],
        compiler_params=_params("arbitrary", "arbitrary"),
        name="cross_attn",
    )(q, mk, mv)


def kernel(x_prompt, x_sample, mem_prompt, state_delta_S, state_delta_conv, state_pool, cache_mem_k, cache_mem_v,
           norm_mix, norm_ffn, norm_mem, norm_final, w_in_delta, conv_w, a_log, dt_bias, delta_onorm,
           w_in_pool, w_pool_grp, pool_scale, w_mem_kv, w_out, w_gate_up, w_down):
    bp, lp, d = x_prompt.shape
    bs, ls, _ = x_sample.shape
    depth = w_out.shape[0]
    n_mem = mem_prompt.shape[1]
    n_heads = state_delta_S.shape[2]
    tok_w = n_heads * HEAD_DIM
    main_w = 4 * tok_w
    d_ff = w_down.shape[1]
    tp, ts = bp * lp, bs * ls
    n_past = min(16384, POOL_HIST - 1)

    x = jnp.concatenate([x_prompt.reshape(tp, d), x_sample.reshape(ts, d)], axis=0)

    mem_rows = mem_prompt.reshape(bp * n_mem, d)
    p_kv = [matmul(rmsnorm(mem_rows, norm_mem[l], BF16), w_mem_kv, l, ncols=2 * MEM_W, tn=MEM_W, out_dtype=F32)
            for l in range(depth)]
    p_mem_k = jnp.stack([kv[:, :MEM_W].reshape(bp, n_mem, N_MEM_HEADS, HEAD_DIM) for kv in p_kv])
    p_mem_v = jnp.stack([kv[:, MEM_W:].reshape(bp, n_mem, N_MEM_HEADS, HEAD_DIM) for kv in p_kv])
    cache_k = cache_mem_k.reshape(depth, bs, n_mem, MEM_W)
    cache_v = cache_mem_v.reshape(depth, bs, n_mem, MEM_W)

    zero_s = jnp.zeros((1, bp, n_heads, HEAD_DIM, HEAD_DIM), F32)
    zero_conv = jnp.zeros((bp, CONV_W - 1, 3 * tok_w), F32)
    zero_pool = jnp.zeros((bp, POOL_HIST - 1, tok_w), F32)

    p_s, p_conv, p_pool, s_s, s_conv, s_pool = [], [], [], [], [], []
    di = pi = 0
    for l in range(depth):
        h = rmsnorm(x, norm_mix[l], BF16)
        if l % 2 == 0:
            proj = matmul(h, w_in_delta, di, ncols=main_w, tn=1024, out_dtype=F32)
            w_tail = w_in_delta[di, :, main_w:]
            w_small = jnp.concatenate(
                [w_tail[:, 2 * n_heads:], w_tail[:, :2 * n_heads], jnp.zeros((d, LANE - 2 * n_heads), F32)], axis=1)
            small = matmul(h, w_small[None], 0, ncols=MEM_W + LANE, tn=MEM_W + LANE, out_dtype=F32)
            y_tok_p, conv_p, s_p = delta_mixer(
                proj, small, zero_conv, zero_s, 0, conv_w[di], a_log[di], dt_bias[di], delta_onorm[di],
                batch=bp, seq=lp, chunk=64)
            small_s = small[tp:].reshape(bs, ls, MEM_W + LANE)
            y_tok_s, conv_s, s_new = delta_mixer(
                proj[tp:].reshape(bs, ls, main_w), small_s, state_delta_conv[di], state_delta_S, di,
                conv_w[di], a_log[di], dt_bias[di], delta_onorm[di], batch=bs, seq=ls, chunk=64)
            p_s.append(s_p); p_conv.append(conv_p); s_s.append(s_new); s_conv.append(conv_s)
            q_p, q_s, q_blk = small, small_s, 0
            di += 1
        else:
            proj = matmul(h, w_in_pool, pi, ncols=tok_w + MEM_W, tn=1024, out_dtype=F32)
            proj_s = proj[tp:].reshape(bs, ls, tok_w + MEM_W)
            y_tok_p, pool_p = pool_mixer(proj, zero_pool, w_pool_grp, pool_scale, pi,
                                         batch=bp, seq=lp, chunk=64, n_past=0)
            y_tok_s, pool_s = pool_mixer(proj_s, state_pool[pi], w_pool_grp, pool_scale, pi,
                                         batch=bs, seq=ls, chunk=64, n_past=n_past)
            p_pool.append(pool_p); s_pool.append(pool_s)
            q_p, q_s, q_blk = proj, proj_s, tok_w // MEM_W
            pi += 1

        kv = p_kv[l].reshape(bp, n_mem, 2 * MEM_W)
        y_mem_p = cross_attn(
            q_p, q_blk, kv, kv,
            (((None, n_mem, MEM_W), lambda b, i: (b, 0, 0)), ((None, n_mem, MEM_W), lambda b, i: (b, 0, 1))),
            batch=bp, seq=lp, tq=512)
        y_mem_s = cross_attn(
            q_s, q_blk, cache_k, cache_v,
            (((None, None, n_mem, MEM_W), lambda b, i, l=l: (l, b, 0, 0)),
             ((None, None, n_mem, MEM_W), lambda b, i, l=l: (l, b, 0, 0))),
            batch=bs, seq=ls, tq=512)

        y = jnp.concatenate([
            jnp.concatenate([y_tok_p, y_mem_p], axis=1),
            jnp.concatenate([y_tok_s.reshape(ts, tok_w), y_mem_s.reshape(ts, MEM_W)], axis=1)], axis=0).astype(BF16)
        x = matmul(y, w_out, l, ncols=d, tn=1024, out_dtype=F32, residual=x)
        h2 = rmsnorm(x, norm_ffn[l], BF16)
        act = matmul(h2, w_gate_up, l, ncols=d_ff, tn=512, out_dtype=BF16, col_blocks=(0, d_ff // 512), swiglu=True)
        x = matmul(act, w_down, l, ncols=d, tn=512, out_dtype=F32, residual=x)

    y = rmsnorm(x, norm_final, F32)
    return (y[:tp].reshape(bp, lp, d), y[tp:].reshape(bs, ls, d),
            jnp.stack(p_s), jnp.stack(p_conv), jnp.stack(p_pool), p_mem_k, p_mem_v,
            jnp.stack(s_s), jnp.stack(s_conv), jnp.stack(s_pool))
```

```python
import functools
import math

import jax
import jax.numpy as jnp
from jax import lax
from jax.experimental import pallas as pl
from jax.experimental.pallas import tpu as pltpu

F32 = jnp.float32
BF16 = jnp.bfloat16
HIGHEST = lax.Precision.HIGHEST

EPS = 1e-6
HEAD_DIM = 128
N_MEM_HEADS = 4
MEM_W = N_MEM_HEADS * HEAD_DIM
CONV_W = 4
POOL_WINDOWS = (2, 4, 8, 16)
POOL_HIST = max(POOL_WINDOWS)
LANE = 128
SUBLANE = 8
CARRY_OFF = SUBLANE - (CONV_W - 1)
VMEM_LIMIT = 56 * 1024 * 1024


def _params(*sem):
    return pltpu.CompilerParams(dimension_semantics=sem, vmem_limit_bytes=VMEM_LIMIT)


def _sigmoid(x):
    return 1.0 / (1.0 + jnp.exp(-x))


def _dot(a, b, precision=None):
    return jnp.dot(a, b, preferred_element_type=F32, precision=precision)


def _dot_nt(a, b, precision=None):
    return lax.dot_general(a, b, (((1,), (1,)), ((), ())), preferred_element_type=F32, precision=precision)


def _dot_tn(a, b, precision=None):
    return lax.dot_general(a, b, (((0,), (0,)), ((), ())), preferred_element_type=F32, precision=precision)


def _bdot(a, b):
    return _dot(a.astype(BF16), b.astype(BF16))


def _bdot_nt(a, b):
    return _dot_nt(a.astype(BF16), b.astype(BF16))


def _bdot_tn(a, b):
    return _dot_tn(a.astype(BF16), b.astype(BF16))


def _rmsnorm_body(x_ref, w_ref, o_ref):
    x = x_ref[...]
    ms = jnp.mean(x * x, axis=-1, keepdims=True)
    o_ref[...] = (x * lax.rsqrt(ms + EPS) * w_ref[...]).astype(o_ref.dtype)


def rmsnorm(x, w, out_dtype, tm=512):
    t, d = x.shape
    return pl.pallas_call(
        _rmsnorm_body,
        grid=(t // tm,),
        in_specs=[pl.BlockSpec((tm, d), lambda i: (i, 0)), pl.BlockSpec((1, d), lambda i: (0, 0))],
        out_specs=pl.BlockSpec((tm, d), lambda i: (i, 0)),
        out_shape=jax.ShapeDtypeStruct((t, d), out_dtype),
        compiler_params=_params("arbitrary"),
        name="rmsnorm",
    )(x, w.reshape(1, d))


def _mm_body(*refs, n_w, tn, has_res, swiglu, out_slices, slab_valid, w_transposed):
    x_ref = refs[0]
    w_refs = refs[1:1 + n_w]
    res_ref = refs[1 + n_w] if has_res else None
    o_ref = refs[-2]
    wb_ref = refs[-1]

    @pl.when(pl.program_id(1) == 0)
    def _cast_weights():
        for i, w_ref in enumerate(w_refs):
            valid = slab_valid[i]
            if w_transposed:
                wb_ref[i * tn:i * tn + valid, :] = w_ref[0:valid, :].astype(BF16)
                if valid < tn:
                    wb_ref[i * tn + valid:(i + 1) * tn, :] = jnp.zeros((tn - valid, wb_ref.shape[1]), BF16)
            else:
                wb_ref[:, i * tn:i * tn + valid] = w_ref[:, 0:valid].astype(BF16)
                if valid < tn:
                    wb_ref[:, i * tn + valid:(i + 1) * tn] = jnp.zeros((wb_ref.shape[0], tn - valid), BF16)

    if w_transposed:
        acc = _dot_nt(x_ref[...], wb_ref[...])
    else:
        acc = _dot(x_ref[...], wb_ref[...])
    if swiglu:
        gate, up = acc[:, :tn], acc[:, tn:]
        acc = gate * _sigmoid(gate) * up
    if out_slices is not None:
        acc = jnp.concatenate([acc[:, lo:hi] for lo, hi in out_slices], axis=1)
    if has_res:
        acc = acc + res_ref[...]
    o_ref[...] = acc.astype(o_ref.dtype)


def matmul(x, w, layer, *, ncols, tn, out_dtype, col_blocks=(0,), residual=None, swiglu=False, out_slices=None,
           w_transposed=False, tm=512):
    t, k = x.shape
    n_w = len(col_blocks)
    to = tn if out_slices is None else sum(hi - lo for lo, hi in out_slices)
    in_specs = [pl.BlockSpec((tm, k), lambda j, m: (m, 0))]
    for cb in col_blocks:
        if w_transposed:
            in_specs.append(pl.BlockSpec((None, tn, k), lambda j, m, cb=cb: (layer, cb + j, 0)))
        else:
            in_specs.append(pl.BlockSpec((None, k, tn), lambda j, m, cb=cb: (layer, 0, cb + j)))
    args = [x] + [w] * n_w
    if residual is not None:
        in_specs.append(pl.BlockSpec((tm, to), lambda j, m: (m, j)))
        args.append(residual)
    n_tiles = ncols // to
    n_total = w.shape[1] if w_transposed else w.shape[2]
    slab_valid = tuple(min(tn, n_total - (cb + n_tiles - 1) * tn) for cb in col_blocks)
    assert all(v == tn for v in slab_valid) or n_tiles == 1
    body = functools.partial(_mm_body, n_w=n_w, tn=tn, has_res=residual is not None, swiglu=swiglu,
                             out_slices=out_slices, slab_valid=slab_valid, w_transposed=w_transposed)
    return pl.pallas_call(
        body,
        grid=(n_tiles, t // tm),
        in_specs=in_specs,
        out_specs=pl.BlockSpec((tm, to), lambda j, m: (m, j)),
        out_shape=jax.ShapeDtypeStruct((t, ncols), out_dtype),
        scratch_shapes=[pltpu.VMEM((n_w * tn, k) if w_transposed else (k, n_w * tn), BF16)],
        compiler_params=_params("arbitrary", "arbitrary"),
        name="matmul",
    )(*args)


def _delta_body(*refs, c_real, c_pad, n_heads, tok_w, head_group):
    main_ref, small_ref, cbuf_ref, s0_ref, cw_ref, alog_ref, dtb_ref, onw_ref = refs[:8]
    y_ref, cnew_ref, s_ref, xp_scr, ab_scr = refs[-5:]
    qkv_w = 3 * tok_w
    c = c_pad
    padded = c_pad > c_real

    @pl.when(pl.program_id(1) == 0)
    def _init():
        s_ref[...] = s0_ref[...]
        xp_scr[CARRY_OFF:SUBLANE, 0:qkv_w] = cbuf_ref[...]
        if padded:
            xp_scr[SUBLANE + c_real:SUBLANE + c, :] = jnp.zeros((c - c_real, xp_scr.shape[1]), F32)
            ab_scr[...] = jnp.zeros(ab_scr.shape, F32)

    xp_scr[SUBLANE:SUBLANE + c_real, :] = main_ref[...]
    ab_scr[0:c_real, :] = small_ref[:, MEM_W:MEM_W + LANE]

    ab = ab_scr[...]
    a_in = ab + dtb_ref[...]
    softplus = jnp.maximum(a_in, 0.0) + jnp.log(1.0 + jnp.exp(-jnp.abs(a_in)))
    g_all = -jnp.exp(alog_ref[...]) * softplus
    beta_all = _sigmoid(ab)
    if padded:
        live = lax.broadcasted_iota(jnp.int32, (c, LANE), 0) < c_real
        g_all = jnp.where(live, g_all, 0.0)
        beta_all = jnp.where(live, beta_all, 0.0)

    rows = lax.broadcasted_iota(jnp.int32, (c, c), 0)
    cols = lax.broadcasted_iota(jnp.int32, (c, c), 1)
    incl = rows >= cols
    strict = rows > cols
    eye_l = (lax.broadcasted_iota(jnp.int32, (LANE, LANE), 0)
             == lax.broadcasted_iota(jnp.int32, (LANE, LANE), 1)).astype(F32)

    gc_all = _dot(incl.astype(F32), g_all, HIGHEST)
    gc_t = _dot_nt(eye_l, gc_all, HIGHEST)
    gam_all = jnp.exp(gc_all)
    g_last = gc_all[c - 1:c, :]
    kdec_all = jnp.exp(g_last - gc_all)
    gend_all = jnp.exp(g_last)

    def conv_slab(col):
        acc = xp_scr[CARRY_OFF:CARRY_OFF + c, col:col + LANE] * cw_ref[0:1, col:col + LANE]
        for i in range(1, CONV_W):
            acc = acc + xp_scr[CARRY_OFF + i:CARRY_OFF + i + c, col:col + LANE] * cw_ref[i:i + 1, col:col + LANE]
        return acc * _sigmoid(acc)

    n_levels = int(math.log2(c))
    for g0 in range(0, n_heads, head_group):
        heads = range(g0, min(g0 + head_group, n_heads))
        q, k, v, beta, gam, decay = {}, {}, {}, {}, {}, {}
        for h in heads:
            lo = h * HEAD_DIM
            qh = conv_slab(lo)
            kh = conv_slab(tok_w + lo)
            v[h] = conv_slab(2 * tok_w + lo)
            q[h] = qh * lax.rsqrt(jnp.sum(qh * qh, axis=-1, keepdims=True) + EPS) * (HEAD_DIM ** -0.5)
            k[h] = kh * lax.rsqrt(jnp.sum(kh * kh, axis=-1, keepdims=True) + EPS)
            beta[h] = beta_all[:, n_heads + h:n_heads + h + 1]
            gam[h] = gam_all[:, h:h + 1]
            diff = gc_all[:, h:h + 1] - gc_t[h:h + 1, :]
            decay[h] = jnp.where(incl, jnp.exp(jnp.where(incl, diff, 0.0)), 0.0)

        qk_kk = {h: _bdot_nt(jnp.concatenate([q[h], k[h]], axis=0), k[h]) for h in heads}

        x_pow = {h: -(jnp.where(strict, qk_kk[h][c:] * decay[h], 0.0) * beta[h]) for h in heads}
        t_off = dict(x_pow)
        for lvl in range(1, n_levels + 1):
            for h in heads:
                if lvl == 1:
                    x_pow[h] = _bdot(x_pow[h], x_pow[h])
                elif lvl < n_levels:
                    both = _bdot(jnp.concatenate([t_off[h], x_pow[h]], axis=0), x_pow[h])
                    t_off[h] = t_off[h] + x_pow[h] + both[:c]
                    x_pow[h] = both[c:]
                else:
                    t_off[h] = t_off[h] + x_pow[h] + _bdot(t_off[h], x_pow[h])

        sol = {}
        for h in heads:
            rhs = jnp.concatenate([v[h] * beta[h], k[h] * (beta[h] * gam[h])], axis=1)
            sol[h] = rhs + _bdot(t_off[h], rhs)

        s_old = {h: s_ref[h] for h in heads}
        from_state = {h: _bdot(jnp.concatenate([sol[h][:, HEAD_DIM:], q[h] * gam[h]], axis=0), s_old[h])
                      for h in heads}
        u = {h: sol[h][:, :HEAD_DIM] - from_state[h][:c] for h in heads}
        o = {h: from_state[h][c:] + _bdot(jnp.where(incl, qk_kk[h][:c] * decay[h], 0.0), u[h]) for h in heads}
        s_new = {h: s_old[h] * gend_all[:, h:h + 1] + _bdot_tn(k[h] * kdec_all[:, h:h + 1], u[h]) for h in heads}
        for h in heads:
            s_ref[h] = s_new[h]

        for h in heads:
            lo = h * HEAD_DIM
            oh = o[h] * lax.rsqrt(jnp.mean(o[h] * o[h], axis=-1, keepdims=True) + EPS) * onw_ref[...]
            z = xp_scr[SUBLANE:SUBLANE + c, qkv_w + lo:qkv_w + lo + HEAD_DIM]
            out = oh * (z * _sigmoid(z))
            y_ref[:, lo:lo + HEAD_DIM] = out[:c_real].astype(y_ref.dtype)

    carry = xp_scr[CARRY_OFF + c_real:SUBLANE + c_real, 0:qkv_w]
    xp_scr[CARRY_OFF:SUBLANE, 0:qkv_w] = carry
    cnew_ref[...] = carry


def delta_mixer(main, small, conv_buf, s0, s_layer, s_out_prev, conv_w, a_log, dt_bias, onorm_w, *,
                batch, seq, chunk, y_dtype, head_group=12):
    n_heads = s0.shape[2]
    n_layers = s0.shape[0]
    tok_w = n_heads * HEAD_DIM
    main_w = 4 * tok_w
    c_real = min(chunk, seq)
    c_pad = max(c_real, SUBLANE)
    nch = seq // c_real
    if main.ndim == 2:
        row_spec = lambda w: pl.BlockSpec((c_real, w), lambda b, ch: (b * nch + ch, 0))
        y_shape = (batch * seq, tok_w)
    else:
        row_spec = lambda w: pl.BlockSpec((None, c_real, w), lambda b, ch: (b, ch, 0))
        y_shape = (batch, seq, tok_w)
    pad_lane = lambda p: jnp.zeros((1, LANE), F32).at[0, :n_heads].set(p.astype(F32))
    body = functools.partial(_delta_body, c_real=c_real, c_pad=c_pad, n_heads=n_heads, tok_w=tok_w,
                             head_group=head_group)
    s_spec = pl.BlockSpec((None, None, n_heads, HEAD_DIM, HEAD_DIM), lambda b, ch: (s_layer, b, 0, 0, 0))
    in_specs = [
        row_spec(main_w),
        row_spec(small.shape[-1]),
        pl.BlockSpec((None, CONV_W - 1, 3 * tok_w), lambda b, ch: (b, 0, 0)),
        s_spec,
        pl.BlockSpec((CONV_W, 3 * tok_w), lambda b, ch: (0, 0)),
        pl.BlockSpec((1, LANE), lambda b, ch: (0, 0)),
        pl.BlockSpec((1, LANE), lambda b, ch: (0, 0)),
        pl.BlockSpec((1, HEAD_DIM), lambda b, ch: (0, 0)),
    ]
    args = [main, small, conv_buf, s0, conv_w, pad_lane(a_log), pad_lane(dt_bias), onorm_w.reshape(1, HEAD_DIM)]
    aliases = {}
    if s_out_prev is not None:
        in_specs.append(pl.BlockSpec(memory_space=pl.ANY))
        args.append(s_out_prev)
        aliases = {len(args) - 1: 2}
    return pl.pallas_call(
        body,
        grid=(batch, nch),
        in_specs=in_specs,
        out_specs=[
            row_spec(tok_w),
            pl.BlockSpec((None, CONV_W - 1, 3 * tok_w), lambda b, ch: (b, 0, 0)),
            s_spec,
        ],
        out_shape=[
            jax.ShapeDtypeStruct(y_shape, y_dtype),
            jax.ShapeDtypeStruct((batch, CONV_W - 1, 3 * tok_w), F32),
            jax.ShapeDtypeStruct((n_layers, batch, n_heads, HEAD_DIM, HEAD_DIM), F32),
        ],
        scratch_shapes=[pltpu.VMEM((SUBLANE + c_pad, main_w), F32), pltpu.VMEM((c_pad, LANE), F32)],
        input_output_aliases=aliases,
        compiler_params=_params("arbitrary", "arbitrary"),
        name="delta_mixer",
    )(*args)


def _pool_body(u_ref, buf_ref, wg_ref, scale_ref, y_ref, pnew_ref, hist_scr, *, c_real, c_pad, n_past, tok_w):
    c = c_pad
    gw = tok_w // len(POOL_WINDOWS)

    @pl.when(pl.program_id(1) == 0)
    def _init():
        hist_scr[0:1, :] = jnp.zeros((1, tok_w), F32)
        hist_scr[1:POOL_HIST, :] = buf_ref[...]
        if c_pad > c_real:
            hist_scr[POOL_HIST + c_real:POOL_HIST + c, :] = jnp.zeros((c - c_real, tok_w), F32)

    hist_scr[POOL_HIST:POOL_HIST + c_real, :] = u_ref[...]
    pos = pl.program_id(1) * c_real + lax.broadcasted_iota(jnp.int32, (c, 1), 0)
    for gi, win in enumerate(POOL_WINDOWS):
        lo = gi * gw
        tok = hist_scr[POOL_HIST:POOL_HIST + c, lo:lo + gw]
        acc = tok
        for s in range(1, win):
            acc = acc + hist_scr[POOL_HIST - s:POOL_HIST - s + c, lo:lo + gw]
        cnt = jnp.minimum(win, pos + 1 + n_past).astype(F32)
        d = acc / cnt - tok
        y = _bdot(d, wg_ref[gi]) * scale_ref[:, lo:lo + gw]
        y_ref[:, lo:lo + gw] = y[:c_real].astype(y_ref.dtype)

    tail = hist_scr[c_real + 1:c_real + POOL_HIST, :]
    hist_scr[1:POOL_HIST, :] = tail
    pnew_ref[...] = tail


def pool_mixer(u, buf, w_grp, scale, layer, *, batch, seq, chunk, n_past, y_dtype):
    tok_w = buf.shape[-1]
    gw = tok_w // len(POOL_WINDOWS)
    c_real = min(chunk, seq)
    c_pad = max(c_real, SUBLANE)
    nch = seq // c_real
    if u.ndim == 2:
        row_spec = pl.BlockSpec((c_real, tok_w), lambda b, ch: (b * nch + ch, 0))
        y_shape = (batch * seq, tok_w)
    else:
        row_spec = pl.BlockSpec((None, c_real, tok_w), lambda b, ch: (b, ch, 0))
        y_shape = (batch, seq, tok_w)
    body = functools.partial(_pool_body, c_real=c_real, c_pad=c_pad, n_past=n_past, tok_w=tok_w)
    return pl.pallas_call(
        body,
        grid=(batch, nch),
        in_specs=[
            row_spec,
            pl.BlockSpec((None, POOL_HIST - 1, tok_w), lambda b, ch: (b, 0, 0)),
            pl.BlockSpec((None, len(POOL_WINDOWS), gw, gw), lambda b, ch: (layer, 0, 0, 0)),
            pl.BlockSpec((None, 1, tok_w), lambda b, ch: (layer, 0, 0)),
        ],
        out_specs=[row_spec, pl.BlockSpec((None, POOL_HIST - 1, tok_w), lambda b, ch: (b, 0, 0))],
        out_shape=[jax.ShapeDtypeStruct(y_shape, y_dtype),
                   jax.ShapeDtypeStruct((batch, POOL_HIST - 1, tok_w), F32)],
        scratch_shapes=[pltpu.VMEM((POOL_HIST + c_pad, tok_w), F32)],
        compiler_params=_params("arbitrary", "arbitrary"),
        name="pool_mixer",
    )(u, buf, w_grp, scale.reshape(scale.shape[0], 1, tok_w))


def _softmax_rows(s):
    p = jnp.exp(s - jnp.max(s, axis=-1, keepdims=True))
    return p / jnp.sum(p, axis=-1, keepdims=True)


def _xattn_prompt_body(q_ref, k_ref, v_ref, o_ref):
    for h in range(N_MEM_HEADS):
        lo = h * HEAD_DIM
        s = _bdot_nt(q_ref[:, lo:lo + HEAD_DIM], k_ref[:, lo:lo + HEAD_DIM]) * (HEAD_DIM ** -0.5)
        o = _bdot(_softmax_rows(s), v_ref[:, lo:lo + HEAD_DIM])
        o_ref[:, lo:lo + HEAD_DIM] = o.astype(o_ref.dtype)


def cross_attn_prompt(q, q_col_block, kv, *, batch, seq, tq, out_dtype):
    n_mem = kv.shape[1]
    tq = min(tq, seq)
    nq = seq // tq
    return pl.pallas_call(
        _xattn_prompt_body,
        grid=(batch, nq),
        in_specs=[
            pl.BlockSpec((tq, MEM_W), lambda b, i: (b * nq + i, q_col_block)),
            pl.BlockSpec((None, n_mem, MEM_W), lambda b, i: (b, 0, 0)),
            pl.BlockSpec((None, n_mem, MEM_W), lambda b, i: (b, 0, 1)),
        ],
        out_specs=pl.BlockSpec((tq, MEM_W), lambda b, i: (b * nq + i, 0)),
        out_shape=jax.ShapeDtypeStruct((batch * seq, MEM_W), out_dtype),
        compiler_params=_params("arbitrary", "arbitrary"),
        name="cross_attn_prompt",
    )(q, kv, kv)


def _xattn_sample_body(q_ref, k_ref, v_ref, o_ref, q_scr, *, bb, seq):
    n_rows = k_ref.shape[1]
    q_scr[...] = jnp.zeros(q_scr.shape, F32)
    q_scr[:, 0:seq, :] = q_ref[...]
    shape = (N_MEM_HEADS * SUBLANE, n_rows)
    same_head = (lax.broadcasted_iota(jnp.int32, shape, 1) % N_MEM_HEADS
                 == lax.broadcasted_iota(jnp.int32, shape, 0) // SUBLANE)
    scores = []
    for b in range(bb):
        qb = q_scr[b]
        q_rows = jnp.concatenate([qb[:, h * HEAD_DIM:(h + 1) * HEAD_DIM] for h in range(N_MEM_HEADS)], axis=0)
        scores.append(_bdot_nt(q_rows, k_ref[b]) * (HEAD_DIM ** -0.5))
    probs = []
    for s in scores:
        p = jnp.exp(s - jnp.max(jnp.where(same_head, s, -jnp.inf), axis=-1, keepdims=True))
        p = jnp.where(same_head, p, 0.0)
        probs.append(p / jnp.sum(p, axis=-1, keepdims=True))
    outs = [_bdot(p, v_ref[b]) for b, p in enumerate(probs)]
    for b, o in enumerate(outs):
        for h in range(N_MEM_HEADS):
            o_ref[b, :, h * HEAD_DIM:(h + 1) * HEAD_DIM] = o[h * SUBLANE:h * SUBLANE + seq].astype(o_ref.dtype)


def cross_attn_sample(q, q_col_block, cache_k, cache_v, layer, *, bb):
    batch, seq, _ = q.shape
    n_rows = cache_k.shape[2]
    kv_spec = pl.BlockSpec((None, bb, n_rows, HEAD_DIM), lambda i: (layer, i, 0, 0))
    return pl.pallas_call(
        functools.partial(_xattn_sample_body, bb=bb, seq=seq),
        grid=(batch // bb,),
        in_specs=[pl.BlockSpec((bb, seq, MEM_W), lambda i: (i, 0, q_col_block)), kv_spec, kv_spec],
        out_specs=pl.BlockSpec((bb, seq, MEM_W), lambda i: (i, 0, 0)),
        out_shape=jax.ShapeDtypeStruct((batch, seq, MEM_W), F32),
        scratch_shapes=[pltpu.VMEM((bb, SUBLANE, MEM_W), F32)],
        compiler_params=_params("arbitrary"),
        name="cross_attn_sample",
    )(q, cache_k, cache_v)


def kernel(x_prompt, x_sample, mem_prompt, state_delta_S, state_delta_conv, state_pool, cache_mem_k, cache_mem_v,
           norm_mix, norm_ffn, norm_mem, norm_final, w_in_delta, conv_w, a_log, dt_bias, delta_onorm,
           w_in_pool, w_pool_grp, pool_scale, w_mem_kv, w_out, w_gate_up, w_down):
    bp, lp, d = x_prompt.shape
    bs, ls, _ = x_sample.shape
    depth = w_out.shape[0]
    n_mem = mem_prompt.shape[1]
    n_heads = state_delta_S.shape[2]
    tok_w = n_heads * HEAD_DIM
    main_w = 4 * tok_w
    d_ff = w_down.shape[1]
    tp, ts = bp * lp, bs * ls
    n_past = min(16384, POOL_HIST - 1)

    x = jnp.concatenate([x_prompt.reshape(tp, d), x_sample.reshape(ts, d)], axis=0)

    mem_rows = mem_prompt.reshape(bp * n_mem, d)
    p_kv = [matmul(rmsnorm(mem_rows, norm_mem[l], BF16), w_mem_kv, l, ncols=2 * MEM_W, tn=MEM_W, out_dtype=F32)
            for l in range(depth)]
    p_mem_k = jnp.stack([kv[:, :MEM_W].reshape(bp, n_mem, N_MEM_HEADS, HEAD_DIM) for kv in p_kv])
    p_mem_v = jnp.stack([kv[:, MEM_W:].reshape(bp, n_mem, N_MEM_HEADS, HEAD_DIM) for kv in p_kv])
    cache_k = cache_mem_k.reshape(depth, bs, n_mem * N_MEM_HEADS, HEAD_DIM)
    cache_v = cache_mem_v.reshape(depth, bs, n_mem * N_MEM_HEADS, HEAD_DIM)

    w_in_delta_t = jnp.swapaxes(w_in_delta, 1, 2)
    n_delta = state_delta_S.shape[0]
    zero_s = jnp.zeros((n_delta, bp, n_heads, HEAD_DIM, HEAD_DIM), F32)
    zero_conv = jnp.zeros((bp, CONV_W - 1, 3 * tok_w), F32)
    zero_pool = jnp.zeros((bp, POOL_HIST - 1, tok_w), F32)

    p_conv, p_pool, s_conv, s_pool = [], [], [], []
    p_s = s_s = None
    di = pi = 0
    for l in range(depth):
        h = rmsnorm(x, norm_mix[l], BF16)
        if l % 2 == 0:
            proj = matmul(h, w_in_delta_t, di, ncols=main_w, tn=1024, out_dtype=F32, w_transposed=True)
            tail0 = main_w // LANE
            small = matmul(h, w_in_delta_t, di, ncols=MEM_W + LANE, tn=LANE, out_dtype=F32, w_transposed=True,
                           col_blocks=tuple(range(tail0, tail0 + (MEM_W + LANE) // LANE)),
                           out_slices=((2 * n_heads, 2 * n_heads + MEM_W), (0, LANE)))
            y_tok_p, conv_p, p_s = delta_mixer(
                proj, small, zero_conv, zero_s, di, p_s, conv_w[di], a_log[di], dt_bias[di], delta_onorm[di],
                batch=bp, seq=lp, chunk=64, y_dtype=BF16)
            small_s = small[tp:].reshape(bs, ls, MEM_W + LANE)
            y_tok_s, conv_s, s_s = delta_mixer(
                proj[tp:].reshape(bs, ls, main_w), small_s, state_delta_conv[di], state_delta_S, di, s_s,
                conv_w[di], a_log[di], dt_bias[di], delta_onorm[di], batch=bs, seq=ls, chunk=64, y_dtype=F32)
            p_conv.append(conv_p); s_conv.append(conv_s)
            q_p, q_s, q_blk = small, small_s, 0
            di += 1
        else:
            proj = matmul(h, w_in_pool, pi, ncols=tok_w + MEM_W, tn=1024, out_dtype=F32)
            proj_s = proj[tp:].reshape(bs, ls, tok_w + MEM_W)
            y_tok_p, pool_p = pool_mixer(proj, zero_pool, w_pool_grp, pool_scale, pi,
                                         batch=bp, seq=lp, chunk=64, n_past=0, y_dtype=BF16)
            y_tok_s, pool_s = pool_mixer(proj_s, state_pool[pi], w_pool_grp, pool_scale, pi,
                                         batch=bs, seq=ls, chunk=64, n_past=n_past, y_dtype=F32)
            p_pool.append(pool_p); s_pool.append(pool_s)
            q_p, q_s, q_blk = proj, proj_s, tok_w // MEM_W
            pi += 1

        y_mem_p = cross_attn_prompt(q_p, q_blk, p_kv[l].reshape(bp, n_mem, 2 * MEM_W),
                                    batch=bp, seq=lp, tq=512, out_dtype=BF16)
        y_mem_s = cross_attn_sample(q_s, q_blk, cache_k, cache_v, l, bb=8)

        y_s = jnp.concatenate([y_tok_s.reshape(ts, tok_w), y_mem_s.reshape(ts, MEM_W)], axis=1).astype(BF16)
        y = jnp.concatenate([jnp.concatenate([y_tok_p, y_mem_p], axis=1), y_s], axis=0)
        x = matmul(y, w_out, l, ncols=d, tn=1024, out_dtype=F32, residual=x)
        h2 = rmsnorm(x, norm_ffn[l], BF16)
        act = matmul(h2, w_gate_up, l, ncols=d_ff, tn=512, out_dtype=BF16, col_blocks=(0, d_ff // 512), swiglu=True)
        x = matmul(act, w_down, l, ncols=d, tn=512, out_dtype=F32, residual=x)

    y = rmsnorm(x, norm_final, F32)
    return (y[:tp].reshape(bp, lp, d), y[tp:].reshape(bs, ls, d),
            p_s, jnp.stack(p_conv), jnp.stack(p_pool), p_mem_k, p_mem_v,
            s_s, jnp.stack(s_conv), jnp.stack(s_pool))
```

```python
import functools
import math

import jax
import jax.numpy as jnp
from jax import lax
from jax.experimental import pallas as pl
from jax.experimental.pallas import tpu as pltpu

F32 = jnp.float32
BF16 = jnp.bfloat16
HIGHEST = lax.Precision.HIGHEST

EPS = 1e-6
HEAD_DIM = 128
N_MEM_HEADS = 4
MEM_W = N_MEM_HEADS * HEAD_DIM
CONV_W = 4
POOL_WINDOWS = (2, 4, 8, 16)
POOL_HIST = max(POOL_WINDOWS)
PAST_LEN = 16384
LANE = 128
SUBLANE = 8
MXU_COLS = 256
CARRY_OFF = SUBLANE - (CONV_W - 1)
VMEM_LIMIT = 56 * 1024 * 1024


def _params(*sem):
    return pltpu.CompilerParams(dimension_semantics=sem, vmem_limit_bytes=VMEM_LIMIT)


def _sigmoid(x):
    return 1.0 / (1.0 + jnp.exp(-x))


def _dot(a, b, precision=None):
    return jnp.dot(a, b, preferred_element_type=F32, precision=precision)


def _dot_nt(a, b, precision=None):
    return lax.dot_general(a, b, (((1,), (1,)), ((), ())), preferred_element_type=F32, precision=precision)


def _dot_tn(a, b, precision=None):
    return lax.dot_general(a, b, (((0,), (0,)), ((), ())), preferred_element_type=F32, precision=precision)


def _bdot(a, b):
    return _dot(a.astype(BF16), b.astype(BF16))


def _bdot_nt(a, b):
    return _dot_nt(a.astype(BF16), b.astype(BF16))


def _bdot_tn(a, b):
    return _dot_tn(a.astype(BF16), b.astype(BF16))


def _alias_args(in_specs, args, prev, out_index):
    if prev is None:
        return {}
    in_specs.append(pl.BlockSpec(memory_space=pl.ANY))
    args.append(prev)
    return {len(args) - 1: out_index}


def _rmsnorm_body(x_ref, w_ref, o_ref):
    x = x_ref[...]
    ms = jnp.mean(x * x, axis=-1, keepdims=True)
    o_ref[...] = (x * lax.rsqrt(ms + EPS) * w_ref[...]).astype(o_ref.dtype)


def rmsnorm(x, w, out_dtype):
    t, d = x.shape
    tm = next(c for c in (512, 256, 128) if t % c == 0)
    return pl.pallas_call(
        _rmsnorm_body,
        grid=(t // tm,),
        in_specs=[pl.BlockSpec((tm, d), lambda i: (i, 0)), pl.BlockSpec((1, d), lambda i: (0, 0))],
        out_specs=pl.BlockSpec((tm, d), lambda i: (i, 0)),
        out_shape=jax.ShapeDtypeStruct((t, d), out_dtype),
        compiler_params=_params("arbitrary"),
        name="rmsnorm",
    )(x, w.reshape(1, d))


def _rmsnorm_split_body(x_ref, w_ref, a_ref, b_ref, *, n_a):
    x = x_ref[...]
    ms = jnp.mean(x * x, axis=-1, keepdims=True)
    y = x * lax.rsqrt(ms + EPS) * w_ref[...]

    @pl.when(pl.program_id(0) < n_a)
    def _first():
        a_ref[...] = y

    @pl.when(pl.program_id(0) >= n_a)
    def _second():
        b_ref[...] = y


def rmsnorm_split(x, w, rows_a):
    t, d = x.shape
    tm = next(c for c in (512, 256, 128) if rows_a % c == 0 and (t - rows_a) % c == 0)
    n_a = rows_a // tm
    return pl.pallas_call(
        functools.partial(_rmsnorm_split_body, n_a=n_a),
        grid=(t // tm,),
        in_specs=[pl.BlockSpec((tm, d), lambda i: (i, 0)), pl.BlockSpec((1, d), lambda i: (0, 0))],
        out_specs=[pl.BlockSpec((tm, d), lambda i: (jnp.minimum(i, n_a - 1), 0)),
                   pl.BlockSpec((tm, d), lambda i: (jnp.maximum(i - n_a, 0), 0))],
        out_shape=[jax.ShapeDtypeStruct((rows_a, d), F32), jax.ShapeDtypeStruct((t - rows_a, d), F32)],
        compiler_params=_params("arbitrary"),
        name="rmsnorm_split",
    )(x, w.reshape(1, d))


def _mm_body(*refs, n_w, tn, has_res, swiglu, out_slices, slab_valid, w_transposed):
    x_ref = refs[0]
    w_refs = refs[1:1 + n_w]
    res_ref = refs[1 + n_w] if has_res else None
    o_ref = refs[-2]
    wb_ref = refs[-1]

    @pl.when(pl.program_id(1) == 0)
    def _cast_weights():
        for i, w_ref in enumerate(w_refs):
            valid = slab_valid[i]
            if w_transposed:
                wb_ref[i * tn:i * tn + valid, :] = w_ref[0:valid, :].astype(BF16)
                if valid < tn:
                    wb_ref[i * tn + valid:(i + 1) * tn, :] = jnp.zeros((tn - valid, wb_ref.shape[1]), BF16)
            else:
                wb_ref[:, i * tn:i * tn + valid] = w_ref[:, 0:valid].astype(BF16)
                if valid < tn:
                    wb_ref[:, i * tn + valid:(i + 1) * tn] = jnp.zeros((wb_ref.shape[0], tn - valid), BF16)

    def product(lo, hi):
        if w_transposed:
            return _dot_nt(x_ref[...], wb_ref[lo:hi, :])
        return _dot(x_ref[...], wb_ref[:, lo:hi])

    if out_slices is not None:
        acc = product(0, n_w * tn)
        acc = jnp.concatenate([acc[:, lo:hi] for lo, hi in out_slices], axis=1)
        o_ref[...] = acc.astype(o_ref.dtype)
        return
    for c0 in range(0, tn, MXU_COLS):
        c1 = min(c0 + MXU_COLS, tn)
        acc = product(c0, c1)
        if swiglu:
            acc = acc * _sigmoid(acc) * product(tn + c0, tn + c1)
        if has_res:
            acc = acc + res_ref[:, c0:c1]
        o_ref[:, c0:c1] = acc.astype(o_ref.dtype)


def matmul(x, w, layer, *, ncols, tn, out_dtype, col_blocks=(0,), residual=None, swiglu=False, out_slices=None,
           w_transposed=False, tm_max=1088):
    t, k = x.shape
    tm = next(c for c in (1088, 1024, 512, 256, 128) if c <= tm_max and t % c == 0)
    n_w = len(col_blocks)
    to = tn if out_slices is None else sum(hi - lo for lo, hi in out_slices)
    in_specs = [pl.BlockSpec((tm, k), lambda j, m: (m, 0))]
    for cb in col_blocks:
        if w_transposed:
            in_specs.append(pl.BlockSpec((None, tn, k), lambda j, m, cb=cb: (layer, cb + j, 0)))
        else:
            in_specs.append(pl.BlockSpec((None, k, tn), lambda j, m, cb=cb: (layer, 0, cb + j)))
    args = [x] + [w] * n_w
    if residual is not None:
        in_specs.append(pl.BlockSpec((tm, to), lambda j, m: (m, j)))
        args.append(residual)
    n_tiles = ncols // to
    n_total = w.shape[1] if w_transposed else w.shape[2]
    slab_valid = tuple(min(tn, n_total - (cb + n_tiles - 1) * tn) for cb in col_blocks)
    assert all(v == tn for v in slab_valid) or n_tiles == 1
    body = functools.partial(_mm_body, n_w=n_w, tn=tn, has_res=residual is not None, swiglu=swiglu,
                             out_slices=out_slices, slab_valid=slab_valid, w_transposed=w_transposed)
    return pl.pallas_call(
        body,
        grid=(n_tiles, t // tm),
        in_specs=in_specs,
        out_specs=pl.BlockSpec((tm, to), lambda j, m: (m, j)),
        out_shape=jax.ShapeDtypeStruct((t, ncols), out_dtype),
        scratch_shapes=[pltpu.VMEM((n_w * tn, k) if w_transposed else (k, n_w * tn), BF16)],
        compiler_params=_params("arbitrary", "arbitrary"),
        name="matmul",
    )(*args)


def _delta_body(*refs, c_real, c_pad, bb, n_heads, tok_w, unit_group, n_alias):
    main_ref, small_ref, cbuf_ref, s0_ref, cw_ref, alog_ref, dtb_ref, onw_ref = refs[:8]
    y_ref, cnew_ref, s_ref, xp_scr, ab_scr, y_scr = refs[8 + n_alias:]
    qkv_w = 3 * tok_w
    c = c_pad
    padded = c_pad > c_real

    @pl.when(pl.program_id(1) == 0)
    def _init():
        s_ref[...] = s0_ref[...]
        for s in range(bb):
            xp_scr[s, CARRY_OFF:SUBLANE, 0:qkv_w] = cbuf_ref[s]
            if padded:
                xp_scr[s, SUBLANE + c_real:SUBLANE + c, :] = jnp.zeros((c - c_real, xp_scr.shape[2]), F32)
        if padded:
            ab_scr[...] = jnp.zeros(ab_scr.shape, F32)

    for s in range(bb):
        xp_scr[s, SUBLANE:SUBLANE + c_real, :] = main_ref[s * c_real:(s + 1) * c_real, :]
        ab_scr[s, 0:c_real, :] = small_ref[s * c_real:(s + 1) * c_real, MEM_W:MEM_W + LANE]

    rows = lax.broadcasted_iota(jnp.int32, (c, c), 0)
    cols = lax.broadcasted_iota(jnp.int32, (c, c), 1)
    incl = rows >= cols
    strict = rows > cols
    eye_l = (lax.broadcasted_iota(jnp.int32, (LANE, LANE), 0)
             == lax.broadcasted_iota(jnp.int32, (LANE, LANE), 1)).astype(F32)

    beta_all, gc_all, gc_t, gam_all, kdec_all, gend_all = [], [], [], [], [], []
    for s in range(bb):
        ab = ab_scr[s]
        a_in = ab + dtb_ref[...]
        softplus = jnp.maximum(a_in, 0.0) + jnp.log(1.0 + jnp.exp(-jnp.abs(a_in)))
        g = -jnp.exp(alog_ref[...]) * softplus
        beta = _sigmoid(ab)
        if padded:
            live = lax.broadcasted_iota(jnp.int32, (c, LANE), 0) < c_real
            g = jnp.where(live, g, 0.0)
            beta = jnp.where(live, beta, 0.0)
        gc = _dot(incl.astype(F32), g, HIGHEST)
        g_last = gc[c - 1:c, :]
        beta_all.append(beta)
        gc_all.append(gc)
        gc_t.append(_dot_nt(eye_l, gc, HIGHEST))
        gam_all.append(jnp.exp(gc))
        kdec_all.append(jnp.exp(g_last - gc))
        gend_all.append(jnp.exp(g_last))

    def conv_slab(s, col):
        acc = xp_scr[s, CARRY_OFF:CARRY_OFF + c, col:col + LANE] * cw_ref[0:1, col:col + LANE]
        for i in range(1, CONV_W):
            acc = acc + (xp_scr[s, CARRY_OFF + i:CARRY_OFF + i + c, col:col + LANE]
                         * cw_ref[i:i + 1, col:col + LANE])
        return acc * _sigmoid(acc)

    n_levels = int(math.log2(c))
    all_units = [(s, h) for s in range(bb) for h in range(n_heads)]
    for g0 in range(0, len(all_units), unit_group):
        units = all_units[g0:g0 + unit_group]
        q, k, v, beta, gam, decay = {}, {}, {}, {}, {}, {}
        for un in units:
            s, h = un
            lo = h * HEAD_DIM
            qh = conv_slab(s, lo)
            kh = conv_slab(s, tok_w + lo)
            v[un] = conv_slab(s, 2 * tok_w + lo)
            q[un] = qh * lax.rsqrt(jnp.sum(qh * qh, axis=-1, keepdims=True) + EPS) * (HEAD_DIM ** -0.5)
            k[un] = kh * lax.rsqrt(jnp.sum(kh * kh, axis=-1, keepdims=True) + EPS)
            beta[un] = beta_all[s][:, n_heads + h:n_heads + h + 1]
            gam[un] = gam_all[s][:, h:h + 1]
            diff = gc_all[s][:, h:h + 1] - gc_t[s][h:h + 1, :]
            decay[un] = jnp.where(incl, jnp.exp(jnp.where(incl, diff, 0.0)), 0.0)

        qk_kk = {un: _bdot_nt(jnp.concatenate([q[un], k[un]], axis=0), k[un]) for un in units}

        x_pow = {un: -(jnp.where(strict, qk_kk[un][c:] * decay[un], 0.0) * beta[un]) for un in units}
        t_off = dict(x_pow)
        for lvl in range(1, n_levels + 1):
            for un in units:
                if lvl == 1:
                    x_pow[un] = _bdot(x_pow[un], x_pow[un])
                elif lvl < n_levels:
                    both = _bdot(jnp.concatenate([t_off[un], x_pow[un]], axis=0), x_pow[un])
                    t_off[un] = t_off[un] + x_pow[un] + both[:c]
                    x_pow[un] = both[c:]
                else:
                    t_off[un] = t_off[un] + x_pow[un] + _bdot(t_off[un], x_pow[un])

        sol = {}
        for un in units:
            rhs = jnp.concatenate([v[un] * beta[un], k[un] * (beta[un] * gam[un])], axis=1)
            sol[un] = rhs + _bdot(t_off[un], rhs)

        s_old = {un: s_ref[un[0], un[1]] for un in units}
        from_state = {un: _bdot(jnp.concatenate([sol[un][:, HEAD_DIM:], q[un] * gam[un]], axis=0), s_old[un])
                      for un in units}
        u = {un: sol[un][:, :HEAD_DIM] - from_state[un][:c] for un in units}
        o = {un: from_state[un][c:] + _bdot(jnp.where(incl, qk_kk[un][:c] * decay[un], 0.0), u[un])
             for un in units}
        s_new = {un: (s_old[un] * gend_all[un[0]][:, un[1]:un[1] + 1]
                      + _bdot_tn(k[un] * kdec_all[un[0]][:, un[1]:un[1] + 1], u[un])) for un in units}
        for un in units:
            s_ref[un[0], un[1]] = s_new[un]

        for un in units:
            s, h = un
            lo = h * HEAD_DIM
            oh = o[un] * lax.rsqrt(jnp.mean(o[un] * o[un], axis=-1, keepdims=True) + EPS) * onw_ref[...]
            z = xp_scr[s, SUBLANE:SUBLANE + c, qkv_w + lo:qkv_w + lo + HEAD_DIM]
            out = oh * (z * _sigmoid(z))
            if padded:
                y_scr[s * c_real:(s + 1) * c_real, lo:lo + HEAD_DIM] = out[:c_real]
            else:
                y_ref[:, lo:lo + HEAD_DIM] = out.astype(y_ref.dtype)

    if padded:
        y_ref[...] = y_scr[...].astype(y_ref.dtype)
    for s in range(bb):
        carry = xp_scr[s, CARRY_OFF + c_real:SUBLANE + c_real, 0:qkv_w]
        xp_scr[s, CARRY_OFF:SUBLANE, 0:qkv_w] = carry
        cnew_ref[s] = carry


def delta_mixer(main, small, row_off, conv_buf, conv_layer, s0, s_layer, s_out_prev, y_prev, y_shape,
                conv_w, a_log, dt_bias, onorm_w, *, batch, seq, chunk, bb, unit_group):
    n_layers, _, n_heads = s0.shape[:3]
    tok_w = n_heads * HEAD_DIM
    main_w = 4 * tok_w
    qkv_w = 3 * tok_w
    c_real = min(chunk, seq)
    c_pad = max(c_real, SUBLANE)
    assert bb == 1 or c_real == seq
    nch = seq // c_real
    rows = bb * c_real
    blk0 = row_off // rows
    row_spec = lambda w: pl.BlockSpec((rows, w), lambda b, ch: (blk0 + b * nch + ch, 0))
    pad_lane = lambda p: jnp.zeros((1, LANE), F32).at[0, :n_heads].set(p.astype(F32))
    s_spec = pl.BlockSpec((None, bb, n_heads, HEAD_DIM, HEAD_DIM), lambda b, ch: (s_layer, b, 0, 0, 0))
    in_specs = [
        row_spec(main_w),
        row_spec(small.shape[-1]),
        pl.BlockSpec((None, bb, CONV_W - 1, qkv_w), lambda b, ch: (conv_layer, b, 0, 0)),
        s_spec,
        pl.BlockSpec((CONV_W, qkv_w), lambda b, ch: (0, 0)),
        pl.BlockSpec((1, LANE), lambda b, ch: (0, 0)),
        pl.BlockSpec((1, LANE), lambda b, ch: (0, 0)),
        pl.BlockSpec((1, HEAD_DIM), lambda b, ch: (0, 0)),
    ]
    args = [main, small, conv_buf, s0, conv_w, pad_lane(a_log), pad_lane(dt_bias), onorm_w.reshape(1, HEAD_DIM)]
    aliases = {}
    aliases.update(_alias_args(in_specs, args, y_prev, 0))
    aliases.update(_alias_args(in_specs, args, s_out_prev, 2))
    body = functools.partial(_delta_body, c_real=c_real, c_pad=c_pad, bb=bb, n_heads=n_heads, tok_w=tok_w,
                             unit_group=unit_group, n_alias=len(aliases))
    return pl.pallas_call(
        body,
        grid=(batch // bb, nch),
        in_specs=in_specs,
        out_specs=[
            row_spec(tok_w),
            pl.BlockSpec((bb, CONV_W - 1, qkv_w), lambda b, ch: (b, 0, 0)),
            s_spec,
        ],
        out_shape=[
            jax.ShapeDtypeStruct(y_shape, BF16),
            jax.ShapeDtypeStruct((batch, CONV_W - 1, qkv_w), F32),
            jax.ShapeDtypeStruct((n_layers, batch, n_heads, HEAD_DIM, HEAD_DIM), F32),
        ],
        scratch_shapes=[pltpu.VMEM((bb, SUBLANE + c_pad, main_w), F32), pltpu.VMEM((bb, c_pad, LANE), F32),
                        pltpu.VMEM((rows, tok_w), F32)],
        input_output_aliases=aliases,
        compiler_params=_params("arbitrary", "arbitrary"),
        name="delta_mixer",
    )(*args)


def _pool_body(*refs, c_real, c_pad, bb, n_past, tok_w, n_alias):
    u_ref, buf_ref, wg_ref, scale_ref = refs[:4]
    y_ref, pnew_ref, hist_scr, y_scr = refs[4 + n_alias:]
    c = c_pad
    padded = c_pad > c_real
    gw = tok_w // len(POOL_WINDOWS)

    @pl.when(pl.program_id(1) == 0)
    def _init():
        for s in range(bb):
            hist_scr[s, 0:1, :] = jnp.zeros((1, tok_w), F32)
            hist_scr[s, 1:POOL_HIST, :] = buf_ref[s]
            if padded:
                hist_scr[s, POOL_HIST + c_real:POOL_HIST + c, :] = jnp.zeros((c - c_real, tok_w), F32)

    for s in range(bb):
        hist_scr[s, POOL_HIST:POOL_HIST + c_real, :] = u_ref[s * c_real:(s + 1) * c_real, :]
    pos = pl.program_id(1) * c_real + lax.broadcasted_iota(jnp.int32, (c, 1), 0)
    for gi, win in enumerate(POOL_WINDOWS):
        lo = gi * gw
        inv_cnt = 1.0 / jnp.minimum(win, pos + 1 + n_past).astype(F32)
        ds = []
        for s in range(bb):
            tok = hist_scr[s, POOL_HIST:POOL_HIST + c, lo:lo + gw]
            acc = tok
            for back in range(1, win):
                acc = acc + hist_scr[s, POOL_HIST - back:POOL_HIST - back + c, lo:lo + gw]
            ds.append(acc * inv_cnt - tok)
        d_all = ds[0] if bb == 1 else jnp.concatenate(ds, axis=0)
        y_all = _bdot(d_all, wg_ref[gi]) * scale_ref[:, lo:lo + gw]
        if padded:
            for s in range(bb):
                y_scr[s * c_real:(s + 1) * c_real, lo:lo + gw] = y_all[s * c:s * c + c_real]
        else:
            y_ref[:, lo:lo + gw] = y_all.astype(y_ref.dtype)

    if padded:
        y_ref[...] = y_scr[...].astype(y_ref.dtype)
    for s in range(bb):
        tail = hist_scr[s, c_real + 1:c_real + POOL_HIST, :]
        hist_scr[s, 1:POOL_HIST, :] = tail
        pnew_ref[s] = tail


def pool_mixer(u, row_off, buf, buf_layer, w_grp, scale, layer, y_prev, y_shape, *, batch, seq, chunk, bb, n_past):
    tok_w = buf.shape[-1]
    gw = tok_w // len(POOL_WINDOWS)
    c_real = min(chunk, seq)
    c_pad = max(c_real, SUBLANE)
    assert bb == 1 or c_real == seq
    nch = seq // c_real
    rows = bb * c_real
    blk0 = row_off // rows
    row_spec = pl.BlockSpec((rows, tok_w), lambda b, ch: (blk0 + b * nch + ch, 0))
    in_specs = [
        row_spec,
        pl.BlockSpec((None, bb, POOL_HIST - 1, tok_w), lambda b, ch: (buf_layer, b, 0, 0)),
        pl.BlockSpec((None, len(POOL_WINDOWS), gw, gw), lambda b, ch: (layer, 0, 0, 0)),
        pl.BlockSpec((None, 1, tok_w), lambda b, ch: (layer, 0, 0)),
    ]
    args = [u, buf, w_grp, scale.reshape(scale.shape[0], 1, tok_w)]
    aliases = _alias_args(in_specs, args, y_prev, 0)
    body = functools.partial(_pool_body, c_real=c_real, c_pad=c_pad, bb=bb, n_past=n_past, tok_w=tok_w,
                             n_alias=len(aliases))
    return pl.pallas_call(
        body,
        grid=(batch // bb, nch),
        in_specs=in_specs,
        out_specs=[row_spec, pl.BlockSpec((bb, POOL_HIST - 1, tok_w), lambda b, ch: (b, 0, 0))],
        out_shape=[jax.ShapeDtypeStruct(y_shape, BF16),
                   jax.ShapeDtypeStruct((batch, POOL_HIST - 1, tok_w), F32)],
        scratch_shapes=[pltpu.VMEM((bb, POOL_HIST + c_pad, tok_w), F32), pltpu.VMEM((rows, tok_w), F32)],
        input_output_aliases=aliases,
        compiler_params=_params("arbitrary", "arbitrary"),
        name="pool_mixer",
    )(*args)


def _softmax_rows(s):
    p = jnp.exp(s - jnp.max(s, axis=-1, keepdims=True))
    return p / jnp.sum(p, axis=-1, keepdims=True)


def _xattn_prompt_body(q_ref, k_ref, v_ref, y_prev_ref, o_ref):
    del y_prev_ref
    for h in range(N_MEM_HEADS):
        lo = h * HEAD_DIM
        s = _bdot_nt(q_ref[:, lo:lo + HEAD_DIM], k_ref[:, lo:lo + HEAD_DIM]) * (HEAD_DIM ** -0.5)
        o = _bdot(_softmax_rows(s), v_ref[:, lo:lo + HEAD_DIM])
        o_ref[:, lo:lo + HEAD_DIM] = o.astype(o_ref.dtype)


def cross_attn_prompt(q, q_col_block, kv, y_prev, *, batch, seq, tq):
    n_mem = kv.shape[1]
    tq = min(tq, seq)
    nq = seq // tq
    y_col_block = y_prev.shape[1] // MEM_W - 1
    return pl.pallas_call(
        _xattn_prompt_body,
        grid=(batch, nq),
        in_specs=[
            pl.BlockSpec((tq, MEM_W), lambda b, i: (b * nq + i, q_col_block)),
            pl.BlockSpec((None, n_mem, MEM_W), lambda b, i: (b, 0, 0)),
            pl.BlockSpec((None, n_mem, MEM_W), lambda b, i: (b, 0, 1)),
            pl.BlockSpec(memory_space=pl.ANY),
        ],
        out_specs=pl.BlockSpec((tq, MEM_W), lambda b, i: (b * nq + i, y_col_block)),
        out_shape=jax.ShapeDtypeStruct(y_prev.shape, y_prev.dtype),
        input_output_aliases={3: 0},
        compiler_params=_params("arbitrary", "arbitrary"),
        name="cross_attn_prompt",
    )(q, kv, kv, y_prev)


def _xattn_sample_body(q_ref, k_ref, v_ref, y_prev_ref, o_ref, q_scr, o_scr, *, bb, seq):
    del y_prev_ref
    n_rows = k_ref.shape[1]
    q_scr[...] = jnp.zeros(q_scr.shape, F32)
    for b in range(bb):
        q_scr[b, 0:seq, :] = q_ref[b * seq:(b + 1) * seq, :]
    shape = (N_MEM_HEADS * SUBLANE, n_rows)
    same_head = (lax.broadcasted_iota(jnp.int32, shape, 1) % N_MEM_HEADS
                 == lax.broadcasted_iota(jnp.int32, shape, 0) // SUBLANE)
    scores = []
    for b in range(bb):
        qb = q_scr[b]
        q_rows = jnp.concatenate([qb[:, h * HEAD_DIM:(h + 1) * HEAD_DIM] for h in range(N_MEM_HEADS)], axis=0)
        scores.append(_bdot_nt(q_rows, k_ref[b]) * (HEAD_DIM ** -0.5))
    probs = []
    for s in scores:
        p = jnp.exp(s - jnp.max(jnp.where(same_head, s, -jnp.inf), axis=-1, keepdims=True))
        p = jnp.where(same_head, p, 0.0)
        probs.append(p / jnp.sum(p, axis=-1, keepdims=True))
    outs = [_bdot(p, v_ref[b]) for b, p in enumerate(probs)]
    for b, o in enumerate(outs):
        for h in range(N_MEM_HEADS):
            o_scr[b * seq:(b + 1) * seq, h * HEAD_DIM:(h + 1) * HEAD_DIM] = o[h * SUBLANE:h * SUBLANE + seq]
    o_ref[...] = o_scr[...].astype(o_ref.dtype)


def cross_attn_sample(q, q_col_block, row_off, cache_k, cache_v, layer, y_prev, *, batch, seq, bb):
    n_rows = cache_k.shape[2]
    rows = bb * seq
    blk0 = row_off // rows
    y_col_block = y_prev.shape[1] // MEM_W - 1
    kv_spec = pl.BlockSpec((None, bb, n_rows, HEAD_DIM), lambda i: (layer, i, 0, 0))
    return pl.pallas_call(
        functools.partial(_xattn_sample_body, bb=bb, seq=seq),
        grid=(batch // bb,),
        in_specs=[pl.BlockSpec((rows, MEM_W), lambda i: (blk0 + i, q_col_block)), kv_spec, kv_spec,
                  pl.BlockSpec(memory_space=pl.ANY)],
        out_specs=pl.BlockSpec((rows, MEM_W), lambda i: (blk0 + i, y_col_block)),
        out_shape=jax.ShapeDtypeStruct(y_prev.shape, y_prev.dtype),
        scratch_shapes=[pltpu.VMEM((bb, SUBLANE, MEM_W), F32), pltpu.VMEM((rows, MEM_W), F32)],
        input_output_aliases={3: 0},
        compiler_params=_params("arbitrary"),
        name="cross_attn_sample",
    )(q, cache_k, cache_v, y_prev)


def kernel(x_prompt, x_sample, mem_prompt, state_delta_S, state_delta_conv, state_pool, cache_mem_k, cache_mem_v,
           norm_mix, norm_ffn, norm_mem, norm_final, w_in_delta, conv_w, a_log, dt_bias, delta_onorm,
           w_in_pool, w_pool_grp, pool_scale, w_mem_kv, w_out, w_gate_up, w_down):
    bp, lp, d = x_prompt.shape
    bs, ls, _ = x_sample.shape
    depth = w_out.shape[0]
    n_mem = mem_prompt.shape[1]
    n_delta, _, n_heads = state_delta_S.shape[:3]
    n_pool = state_pool.shape[0]
    tok_w = n_heads * HEAD_DIM
    main_w = 4 * tok_w
    d_ff = w_down.shape[1]
    tp, ts = bp * lp, bs * ls
    n_past = min(PAST_LEN, POOL_HIST - 1)
    y_shape = (tp + ts, tok_w + MEM_W)

    x = jnp.concatenate([x_prompt.reshape(tp, d), x_sample.reshape(ts, d)], axis=0)

    mem_rows = mem_prompt.reshape(bp * n_mem, d)
    p_kv = [matmul(rmsnorm(mem_rows, norm_mem[l], BF16), w_mem_kv, l, ncols=2 * MEM_W, tn=MEM_W, out_dtype=F32)
            for l in range(depth)]
    p_mem_k = jnp.stack([kv[:, :MEM_W].reshape(bp, n_mem, N_MEM_HEADS, HEAD_DIM) for kv in p_kv])
    p_mem_v = jnp.stack([kv[:, MEM_W:].reshape(bp, n_mem, N_MEM_HEADS, HEAD_DIM) for kv in p_kv])
    cache_k = cache_mem_k.reshape(depth, bs, n_mem * N_MEM_HEADS, HEAD_DIM)
    cache_v = cache_mem_v.reshape(depth, bs, n_mem * N_MEM_HEADS, HEAD_DIM)

    w_in_delta_t = jnp.swapaxes(w_in_delta, 1, 2)
    zero_s = jnp.zeros((n_delta, bp, n_heads, HEAD_DIM, HEAD_DIM), F32)
    zero_conv = jnp.zeros((1, bp, CONV_W - 1, 3 * tok_w), F32)
    zero_pool = jnp.zeros((1, bp, POOL_HIST - 1, tok_w), F32)

    p_conv, p_pool, s_conv, s_pool = [], [], [], []
    p_s = s_s = None
    di = pi = 0
    for l in range(depth):
        h = rmsnorm(x, norm_mix[l], BF16)
        if l % 2 == 0:
            proj = matmul(h, w_in_delta_t, di, ncols=main_w, tn=1024, out_dtype=F32, w_transposed=True)
            tail0 = main_w // LANE
            small = matmul(h, w_in_delta_t, di, ncols=MEM_W + LANE, tn=LANE, out_dtype=F32, w_transposed=True,
                           col_blocks=tuple(range(tail0, tail0 + (MEM_W + LANE) // LANE)),
                           out_slices=((2 * n_heads, 2 * n_heads + MEM_W), (0, LANE)))
            gates = (conv_w[di], a_log[di], dt_bias[di], delta_onorm[di])
            y, conv_p, p_s = delta_mixer(proj, small, 0, zero_conv, 0, zero_s, di, p_s, None, y_shape, *gates,
                                         batch=bp, seq=lp, chunk=64, bb=1, unit_group=n_heads)
            y, conv_s, s_s = delta_mixer(proj, small, tp, state_delta_conv, di, state_delta_S, di, s_s, y, y_shape,
                                         *gates, batch=bs, seq=ls, chunk=64, bb=4, unit_group=2 * n_heads)
            p_conv.append(conv_p); s_conv.append(conv_s)
            q_src, q_blk = small, 0
            di += 1
        else:
            proj = matmul(h, w_in_pool, pi, ncols=tok_w + MEM_W, tn=1024, out_dtype=F32)
            y, pool_p = pool_mixer(proj, 0, zero_pool, 0, w_pool_grp, pool_scale, pi, None, y_shape,
                                   batch=bp, seq=lp, chunk=128, bb=1, n_past=0)
            y, pool_s = pool_mixer(proj, tp, state_pool, pi, w_pool_grp, pool_scale, pi, y, y_shape,
                                   batch=bs, seq=ls, chunk=128, bb=16, n_past=n_past)
            p_pool.append(pool_p); s_pool.append(pool_s)
            q_src, q_blk = proj, tok_w // MEM_W
            pi += 1

        y = cross_attn_prompt(q_src, q_blk, p_kv[l].reshape(bp, n_mem, 2 * MEM_W), y, batch=bp, seq=lp, tq=512)
        y = cross_attn_sample(q_src, q_blk, tp, cache_k, cache_v, l, y, batch=bs, seq=ls, bb=8)

        x = matmul(y, w_out, l, ncols=d, tn=1024, out_dtype=F32, residual=x)
        h2 = rmsnorm(x, norm_ffn[l], BF16)
        act = matmul(h2, w_gate_up, l, ncols=d_ff, tn=512, out_dtype=BF16, col_blocks=(0, d_ff // 512), swiglu=True)
        x = matmul(act, w_down, l, ncols=d, tn=512, out_dtype=F32, residual=x, tm_max=512)

    y_p, y_s = rmsnorm_split(x, norm_final, tp)
    return (y_p.reshape(bp, lp, d), y_s.reshape(bs, ls, d),
            p_s, jnp.stack(p_conv), jnp.stack(p_pool), p_mem_k, p_mem_v,
            s_s, jnp.stack(s_conv), jnp.stack(s_pool))
```

```python
import functools
import math

import jax
import jax.numpy as jnp
from jax import lax
from jax.experimental import pallas as pl
from jax.experimental.pallas import tpu as pltpu

F32 = jnp.float32
BF16 = jnp.bfloat16
HIGHEST = lax.Precision.HIGHEST

EPS = 1e-6
HEAD_DIM = 128
N_MEM_HEADS = 4
MEM_W = N_MEM_HEADS * HEAD_DIM
CONV_W = 4
POOL_WINDOWS = (2, 4, 8, 16)
POOL_HIST = max(POOL_WINDOWS)
PAST_LEN = 16384
LANE = 128
SUBLANE = 8
MXU_COLS = 256
CARRY_OFF = SUBLANE - (CONV_W - 1)
VMEM_LIMIT = 56 * 1024 * 1024


def _params(*sem):
    return pltpu.CompilerParams(dimension_semantics=sem, vmem_limit_bytes=VMEM_LIMIT)


def _sigmoid(x):
    return 1.0 / (1.0 + jnp.exp(-x))


def _dot(a, b, precision=None):
    return jnp.dot(a, b, preferred_element_type=F32, precision=precision)


def _dot_nt(a, b, precision=None):
    return lax.dot_general(a, b, (((1,), (1,)), ((), ())), preferred_element_type=F32, precision=precision)


def _dot_tn(a, b, precision=None):
    return lax.dot_general(a, b, (((0,), (0,)), ((), ())), preferred_element_type=F32, precision=precision)


def _bdot(a, b):
    return _dot(a.astype(BF16), b.astype(BF16))


def _bdot_nt(a, b):
    return _dot_nt(a.astype(BF16), b.astype(BF16))


def _bdot_tn(a, b):
    return _dot_tn(a.astype(BF16), b.astype(BF16))


def _alias_args(in_specs, args, prev, out_index):
    if prev is None:
        return {}
    in_specs.append(pl.BlockSpec(memory_space=pl.ANY))
    args.append(prev)
    return {len(args) - 1: out_index}


def _rmsnorm_body(x_ref, w_ref, o_ref):
    x = x_ref[...]
    ms = jnp.mean(x * x, axis=-1, keepdims=True)
    o_ref[...] = (x * lax.rsqrt(ms + EPS) * w_ref[...]).astype(o_ref.dtype)


def rmsnorm(x, w, out_dtype):
    t, d = x.shape
    tm = next(c for c in (512, 256, 128) if t % c == 0)
    return pl.pallas_call(
        _rmsnorm_body,
        grid=(t // tm,),
        in_specs=[pl.BlockSpec((tm, d), lambda i: (i, 0)), pl.BlockSpec((1, d), lambda i: (0, 0))],
        out_specs=pl.BlockSpec((tm, d), lambda i: (i, 0)),
        out_shape=jax.ShapeDtypeStruct((t, d), out_dtype),
        compiler_params=_params("arbitrary"),
        name="rmsnorm",
    )(x, w.reshape(1, d))


def _rmsnorm_split_body(x_ref, w_ref, a_ref, b_ref, *, n_a):
    x = x_ref[...]
    ms = jnp.mean(x * x, axis=-1, keepdims=True)
    y = x * lax.rsqrt(ms + EPS) * w_ref[...]

    @pl.when(pl.program_id(0) < n_a)
    def _first():
        a_ref[...] = y

    @pl.when(pl.program_id(0) >= n_a)
    def _second():
        b_ref[...] = y


def rmsnorm_split(x, w, rows_a):
    t, d = x.shape
    tm = next(c for c in (512, 256, 128) if rows_a % c == 0 and (t - rows_a) % c == 0)
    n_a = rows_a // tm
    return pl.pallas_call(
        functools.partial(_rmsnorm_split_body, n_a=n_a),
        grid=(t // tm,),
        in_specs=[pl.BlockSpec((tm, d), lambda i: (i, 0)), pl.BlockSpec((1, d), lambda i: (0, 0))],
        out_specs=[pl.BlockSpec((tm, d), lambda i: (jnp.minimum(i, n_a - 1), 0)),
                   pl.BlockSpec((tm, d), lambda i: (jnp.maximum(i - n_a, 0), 0))],
        out_shape=[jax.ShapeDtypeStruct((rows_a, d), F32), jax.ShapeDtypeStruct((t - rows_a, d), F32)],
        compiler_params=_params("arbitrary"),
        name="rmsnorm_split",
    )(x, w.reshape(1, d))


def _mm_body(*refs, n_w, tn, has_res, swiglu, out_slices, slab_valid, w_transposed):
    x_ref = refs[0]
    w_refs = refs[1:1 + n_w]
    res_ref = refs[1 + n_w] if has_res else None
    o_ref = refs[-2]
    wb_ref = refs[-1]

    @pl.when(pl.program_id(1) == 0)
    def _cast_weights():
        for i, w_ref in enumerate(w_refs):
            valid = slab_valid[i]
            if w_transposed:
                wb_ref[i * tn:i * tn + valid, :] = w_ref[0:valid, :].astype(BF16)
                if valid < tn:
                    wb_ref[i * tn + valid:(i + 1) * tn, :] = jnp.zeros((tn - valid, wb_ref.shape[1]), BF16)
            else:
                wb_ref[:, i * tn:i * tn + valid] = w_ref[:, 0:valid].astype(BF16)
                if valid < tn:
                    wb_ref[:, i * tn + valid:(i + 1) * tn] = jnp.zeros((wb_ref.shape[0], tn - valid), BF16)

    def product(lo, hi):
        if w_transposed:
            return _dot_nt(x_ref[...], wb_ref[lo:hi, :])
        return _dot(x_ref[...], wb_ref[:, lo:hi])

    if out_slices is not None:
        acc = product(0, n_w * tn)
        acc = jnp.concatenate([acc[:, lo:hi] for lo, hi in out_slices], axis=1)
        o_ref[...] = acc.astype(o_ref.dtype)
        return
    for c0 in range(0, tn, MXU_COLS):
        c1 = min(c0 + MXU_COLS, tn)
        acc = product(c0, c1)
        if swiglu:
            acc = acc * _sigmoid(acc) * product(tn + c0, tn + c1)
        if has_res:
            acc = acc + res_ref[:, c0:c1]
        o_ref[:, c0:c1] = acc.astype(o_ref.dtype)


def matmul(x, w, layer, *, ncols, tn, out_dtype, col_blocks=(0,), residual=None, swiglu=False, out_slices=None,
           w_transposed=False, tm_max=1088):
    t, k = x.shape
    tm = next(c for c in (1088, 1024, 512, 256, 128) if c <= tm_max and t % c == 0)
    n_w = len(col_blocks)
    to = tn if out_slices is None else sum(hi - lo for lo, hi in out_slices)
    in_specs = [pl.BlockSpec((tm, k), lambda j, m: (m, 0))]
    for cb in col_blocks:
        if w_transposed:
            in_specs.append(pl.BlockSpec((None, tn, k), lambda j, m, cb=cb: (layer, cb + j, 0)))
        else:
            in_specs.append(pl.BlockSpec((None, k, tn), lambda j, m, cb=cb: (layer, 0, cb + j)))
    args = [x] + [w] * n_w
    if residual is not None:
        in_specs.append(pl.BlockSpec((tm, to), lambda j, m: (m, j)))
        args.append(residual)
    n_tiles = ncols // to
    n_total = w.shape[1] if w_transposed else w.shape[2]
    slab_valid = tuple(min(tn, n_total - (cb + n_tiles - 1) * tn) for cb in col_blocks)
    assert all(v == tn for v in slab_valid) or n_tiles == 1
    body = functools.partial(_mm_body, n_w=n_w, tn=tn, has_res=residual is not None, swiglu=swiglu,
                             out_slices=out_slices, slab_valid=slab_valid, w_transposed=w_transposed)
    return pl.pallas_call(
        body,
        grid=(n_tiles, t // tm),
        in_specs=in_specs,
        out_specs=pl.BlockSpec((tm, to), lambda j, m: (m, j)),
        out_shape=jax.ShapeDtypeStruct((t, ncols), out_dtype),
        scratch_shapes=[pltpu.VMEM((n_w * tn, k) if w_transposed else (k, n_w * tn), BF16)],
        compiler_params=_params("arbitrary", "arbitrary"),
        name="matmul",
    )(*args)


def _proj_norm_body(x_ref, w_ref, res_ref, nw_ref, o_ref, h_ref, wb_ref):
    @pl.when(pl.program_id(0) == 0)
    def _cast_weights():
        wb_ref[...] = w_ref[...].astype(BF16)

    n = o_ref.shape[1]
    sumsq = jnp.zeros((o_ref.shape[0], LANE), F32)
    for c0 in range(0, n, MXU_COLS):
        acc = _dot(x_ref[...], wb_ref[:, c0:c0 + MXU_COLS]) + res_ref[:, c0:c0 + MXU_COLS]
        o_ref[:, c0:c0 + MXU_COLS] = acc
        sq = acc * acc
        for l0 in range(0, MXU_COLS, LANE):
            sumsq = sumsq + sq[:, l0:l0 + LANE]
    scale = lax.rsqrt(jnp.sum(sumsq, axis=-1, keepdims=True) / n + EPS)
    for c0 in range(0, n, MXU_COLS):
        h = o_ref[:, c0:c0 + MXU_COLS] * scale * nw_ref[:, c0:c0 + MXU_COLS]
        h_ref[:, c0:c0 + MXU_COLS] = h.astype(h_ref.dtype)


def matmul_residual_norm(x, w, layer, residual, norm_w, *, tm):
    t, k = x.shape
    n = w.shape[2]
    return pl.pallas_call(
        _proj_norm_body,
        grid=(t // tm,),
        in_specs=[
            pl.BlockSpec((tm, k), lambda m: (m, 0)),
            pl.BlockSpec((None, k, n), lambda m: (layer, 0, 0), pipeline_mode=pl.Buffered(1)),
            pl.BlockSpec((tm, n), lambda m: (m, 0)),
            pl.BlockSpec((1, n), lambda m: (0, 0)),
        ],
        out_specs=[pl.BlockSpec((tm, n), lambda m: (m, 0)), pl.BlockSpec((tm, n), lambda m: (m, 0))],
        out_shape=[jax.ShapeDtypeStruct((t, n), F32), jax.ShapeDtypeStruct((t, n), BF16)],
        scratch_shapes=[pltpu.VMEM((k, n), BF16)],
        compiler_params=_params("arbitrary"),
        name="matmul_residual_norm",
    )(x, w, residual, norm_w.reshape(1, n))


def _delta_body(*refs, c_real, c_pad, bb, n_heads, tok_w, unit_group, n_alias):
    main_ref, small_ref, cbuf_ref, s0_ref, cw_ref, alog_ref, dtb_ref, onw_ref = refs[:8]
    y_ref, cnew_ref, s_ref, xp_scr, ab_scr, y_scr = refs[8 + n_alias:]
    qkv_w = 3 * tok_w
    c = c_pad
    padded = c_pad > c_real

    @pl.when(pl.program_id(1) == 0)
    def _init():
        s_ref[...] = s0_ref[...]
        for s in range(bb):
            xp_scr[s, 0:CARRY_OFF, :] = jnp.zeros((CARRY_OFF, xp_scr.shape[2]), F32)
            xp_scr[s, CARRY_OFF:SUBLANE, 0:qkv_w] = cbuf_ref[s]
            if padded:
                xp_scr[s, SUBLANE + c_real:SUBLANE + c, :] = jnp.zeros((c - c_real, xp_scr.shape[2]), F32)
        if padded:
            ab_scr[...] = jnp.zeros(ab_scr.shape, F32)

    for s in range(bb):
        xp_scr[s, SUBLANE:SUBLANE + c_real, :] = main_ref[s * c_real:(s + 1) * c_real, :]
        ab_scr[s, 0:c_real, :] = small_ref[s * c_real:(s + 1) * c_real, MEM_W:MEM_W + LANE]

    rows = lax.broadcasted_iota(jnp.int32, (c, c), 0)
    cols = lax.broadcasted_iota(jnp.int32, (c, c), 1)
    incl = rows >= cols
    strict = rows > cols
    eye_l = (lax.broadcasted_iota(jnp.int32, (LANE, LANE), 0)
             == lax.broadcasted_iota(jnp.int32, (LANE, LANE), 1)).astype(F32)

    beta_all, gc_all, gc_t, gam_all, kdec_all, gend_all = [], [], [], [], [], []
    for s in range(bb):
        ab = ab_scr[s]
        a_in = ab + dtb_ref[...]
        softplus = jnp.maximum(a_in, 0.0) + jnp.log(1.0 + jnp.exp(-jnp.abs(a_in)))
        g = -jnp.exp(alog_ref[...]) * softplus
        beta = _sigmoid(ab)
        if padded:
            live = lax.broadcasted_iota(jnp.int32, (c, LANE), 0) < c_real
            g = jnp.where(live, g, 0.0)
            beta = jnp.where(live, beta, 0.0)
        gc = _dot(incl.astype(F32), g, HIGHEST)
        g_last = gc[c - 1:c, :]
        beta_all.append(beta)
        gc_all.append(gc)
        gc_t.append(_dot_nt(eye_l, gc, HIGHEST))
        gam_all.append(jnp.exp(gc))
        kdec_all.append(jnp.exp(g_last - gc))
        gend_all.append(jnp.exp(g_last))

    def conv_slab(s, col):
        blk = xp_scr[s, :, col:col + LANE]
        n = SUBLANE + c
        acc = blk[SUBLANE:] * cw_ref[CONV_W - 1:CONV_W, col:col + LANE]
        for i in range(CONV_W - 1):
            tap = pltpu.roll(blk, n - (CARRY_OFF + i), axis=0)[:c]
            acc = acc + tap * cw_ref[i:i + 1, col:col + LANE]
        return acc * _sigmoid(acc)

    n_levels = int(math.log2(c))
    all_units = [(s, h) for s in range(bb) for h in range(n_heads)]
    for g0 in range(0, len(all_units), unit_group):
        units = all_units[g0:g0 + unit_group]
        q, k, v, beta, gam, decay = {}, {}, {}, {}, {}, {}
        for un in units:
            s, h = un
            lo = h * HEAD_DIM
            qh = conv_slab(s, lo)
            kh = conv_slab(s, tok_w + lo)
            v[un] = conv_slab(s, 2 * tok_w + lo)
            q[un] = qh * lax.rsqrt(jnp.sum(qh * qh, axis=-1, keepdims=True) + EPS) * (HEAD_DIM ** -0.5)
            k[un] = kh * lax.rsqrt(jnp.sum(kh * kh, axis=-1, keepdims=True) + EPS)
            beta[un] = beta_all[s][:, n_heads + h:n_heads + h + 1]
            gam[un] = gam_all[s][:, h:h + 1]
            diff = gc_all[s][:, h:h + 1] - gc_t[s][h:h + 1, :]
            decay[un] = jnp.where(incl, jnp.exp(jnp.where(incl, diff, 0.0)), 0.0)

        qk_kk = {un: _bdot_nt(jnp.concatenate([q[un], k[un]], axis=0), k[un]) for un in units}

        x_pow = {un: -(jnp.where(strict, qk_kk[un][c:] * decay[un], 0.0) * beta[un]) for un in units}
        t_off = dict(x_pow)
        for lvl in range(1, n_levels + 1):
            for un in units:
                if lvl == 1:
                    x_pow[un] = _bdot(x_pow[un], x_pow[un])
                elif lvl < n_levels:
                    both = _bdot(jnp.concatenate([t_off[un], x_pow[un]], axis=0), x_pow[un])
                    t_off[un] = t_off[un] + x_pow[un] + both[:c]
                    x_pow[un] = both[c:]
                else:
                    t_off[un] = t_off[un] + x_pow[un] + _bdot(t_off[un], x_pow[un])

        sol = {}
        for un in units:
            rhs = jnp.concatenate([v[un] * beta[un], k[un] * (beta[un] * gam[un])], axis=1)
            sol[un] = rhs + _bdot(t_off[un], rhs)

        s_old = {un: s_ref[un[0], un[1]] for un in units}
        from_state = {un: _bdot(jnp.concatenate([sol[un][:, HEAD_DIM:], q[un] * gam[un]], axis=0), s_old[un])
                      for un in units}
        u = {un: sol[un][:, :HEAD_DIM] - from_state[un][:c] for un in units}
        o = {un: from_state[un][c:] + _bdot(jnp.where(incl, qk_kk[un][:c] * decay[un], 0.0), u[un])
             for un in units}
        s_new = {un: (s_old[un] * gend_all[un[0]][:, un[1]:un[1] + 1]
                      + _bdot_tn(k[un] * kdec_all[un[0]][:, un[1]:un[1] + 1], u[un])) for un in units}
        for un in units:
            s_ref[un[0], un[1]] = s_new[un]

        for un in units:
            s, h = un
            lo = h * HEAD_DIM
            oh = o[un] * lax.rsqrt(jnp.mean(o[un] * o[un], axis=-1, keepdims=True) + EPS) * onw_ref[...]
            z = xp_scr[s, SUBLANE:SUBLANE + c, qkv_w + lo:qkv_w + lo + HEAD_DIM]
            out = oh * (z * _sigmoid(z))
            if padded:
                y_scr[s * c_real:(s + 1) * c_real, lo:lo + HEAD_DIM] = out[:c_real]
            else:
                y_ref[:, lo:lo + HEAD_DIM] = out.astype(y_ref.dtype)

    if padded:
        y_ref[...] = y_scr[...].astype(y_ref.dtype)
    for s in range(bb):
        carry = xp_scr[s, CARRY_OFF + c_real:SUBLANE + c_real, 0:qkv_w]
        xp_scr[s, CARRY_OFF:SUBLANE, 0:qkv_w] = carry
        cnew_ref[s] = carry


def delta_mixer(main, small, row_off, conv_buf, conv_layer, s0, s_layer, s_out_prev, y_prev, y_shape,
                conv_w, a_log, dt_bias, onorm_w, *, batch, seq, chunk, bb, unit_group):
    n_layers, _, n_heads = s0.shape[:3]
    tok_w = n_heads * HEAD_DIM
    main_w = 4 * tok_w
    qkv_w = 3 * tok_w
    c_real = min(chunk, seq)
    c_pad = max(c_real, SUBLANE)
    assert bb == 1 or c_real == seq
    nch = seq // c_real
    rows = bb * c_real
    blk0 = row_off // rows
    row_spec = lambda w: pl.BlockSpec((rows, w), lambda b, ch: (blk0 + b * nch + ch, 0))
    pad_lane = lambda p: jnp.zeros((1, LANE), F32).at[0, :n_heads].set(p.astype(F32))
    s_spec = pl.BlockSpec((None, bb, n_heads, HEAD_DIM, HEAD_DIM), lambda b, ch: (s_layer, b, 0, 0, 0))
    in_specs = [
        row_spec(main_w),
        row_spec(small.shape[-1]),
        pl.BlockSpec((None, bb, CONV_W - 1, qkv_w), lambda b, ch: (conv_layer, b, 0, 0)),
        s_spec,
        pl.BlockSpec((CONV_W, qkv_w), lambda b, ch: (0, 0)),
        pl.BlockSpec((1, LANE), lambda b, ch: (0, 0)),
        pl.BlockSpec((1, LANE), lambda b, ch: (0, 0)),
        pl.BlockSpec((1, HEAD_DIM), lambda b, ch: (0, 0)),
    ]
    args = [main, small, conv_buf, s0, conv_w, pad_lane(a_log), pad_lane(dt_bias), onorm_w.reshape(1, HEAD_DIM)]
    aliases = {}
    aliases.update(_alias_args(in_specs, args, y_prev, 0))
    aliases.update(_alias_args(in_specs, args, s_out_prev, 2))
    body = functools.partial(_delta_body, c_real=c_real, c_pad=c_pad, bb=bb, n_heads=n_heads, tok_w=tok_w,
                             unit_group=unit_group, n_alias=len(aliases))
    return pl.pallas_call(
        body,
        grid=(batch // bb, nch),
        in_specs=in_specs,
        out_specs=[
            row_spec(tok_w),
            pl.BlockSpec((bb, CONV_W - 1, qkv_w), lambda b, ch: (b, 0, 0)),
            s_spec,
        ],
        out_shape=[
            jax.ShapeDtypeStruct(y_shape, BF16),
            jax.ShapeDtypeStruct((batch, CONV_W - 1, qkv_w), F32),
            jax.ShapeDtypeStruct((n_layers, batch, n_heads, HEAD_DIM, HEAD_DIM), F32),
        ],
        scratch_shapes=[pltpu.VMEM((bb, SUBLANE + c_pad, main_w), F32), pltpu.VMEM((bb, c_pad, LANE), F32),
                        pltpu.VMEM((rows, tok_w), F32)],
        input_output_aliases=aliases,
        compiler_params=_params("arbitrary", "arbitrary"),
        name="delta_mixer",
    )(*args)


def _pool_body(*refs, c_real, c_pad, bb, n_past, tok_w, n_alias):
    u_ref, buf_ref, wg_ref, scale_ref = refs[:4]
    y_ref, pnew_ref, hist_scr, y_scr = refs[4 + n_alias:]
    c = c_pad
    padded = c_pad > c_real
    gw = tok_w // len(POOL_WINDOWS)

    @pl.when(pl.program_id(1) == 0)
    def _init():
        for s in range(bb):
            hist_scr[s, 0:1, :] = jnp.zeros((1, tok_w), F32)
            hist_scr[s, 1:POOL_HIST, :] = buf_ref[s]
            if padded:
                hist_scr[s, POOL_HIST + c_real:POOL_HIST + c, :] = jnp.zeros((c - c_real, tok_w), F32)

    for s in range(bb):
        hist_scr[s, POOL_HIST:POOL_HIST + c_real, :] = u_ref[s * c_real:(s + 1) * c_real, :]
    pos = pl.program_id(1) * c_real + lax.broadcasted_iota(jnp.int32, (c, 1), 0)
    for gi, win in enumerate(POOL_WINDOWS):
        lo = gi * gw
        inv_cnt = 1.0 / jnp.minimum(win, pos + 1 + n_past).astype(F32)
        ds = []
        for s in range(bb):
            tok = hist_scr[s, POOL_HIST:POOL_HIST + c, lo:lo + gw]
            acc = tok
            for back in range(1, win):
                acc = acc + hist_scr[s, POOL_HIST - back:POOL_HIST - back + c, lo:lo + gw]
            ds.append(acc * inv_cnt - tok)
        d_all = ds[0] if bb == 1 else jnp.concatenate(ds, axis=0)
        y_all = _bdot(d_all, wg_ref[gi]) * scale_ref[:, lo:lo + gw]
        if padded:
            for s in range(bb):
                y_scr[s * c_real:(s + 1) * c_real, lo:lo + gw] = y_all[s * c:s * c + c_real]
        else:
            y_ref[:, lo:lo + gw] = y_all.astype(y_ref.dtype)

    if padded:
        y_ref[...] = y_scr[...].astype(y_ref.dtype)
    for s in range(bb):
        tail = hist_scr[s, c_real + 1:c_real + POOL_HIST, :]
        hist_scr[s, 1:POOL_HIST, :] = tail
        pnew_ref[s] = tail


def pool_mixer(u, row_off, buf, buf_layer, w_grp, scale, layer, y_prev, y_shape, *, batch, seq, chunk, bb, n_past):
    tok_w = buf.shape[-1]
    gw = tok_w // len(POOL_WINDOWS)
    c_real = min(chunk, seq)
    c_pad = max(c_real, SUBLANE)
    assert bb == 1 or c_real == seq
    nch = seq // c_real
    rows = bb * c_real
    blk0 = row_off // rows
    row_spec = pl.BlockSpec((rows, tok_w), lambda b, ch: (blk0 + b * nch + ch, 0))
    in_specs = [
        row_spec,
        pl.BlockSpec((None, bb, POOL_HIST - 1, tok_w), lambda b, ch: (buf_layer, b, 0, 0)),
        pl.BlockSpec((None, len(POOL_WINDOWS), gw, gw), lambda b, ch: (layer, 0, 0, 0)),
        pl.BlockSpec((None, 1, tok_w), lambda b, ch: (layer, 0, 0)),
    ]
    args = [u, buf, w_grp, scale.reshape(scale.shape[0], 1, tok_w)]
    aliases = _alias_args(in_specs, args, y_prev, 0)
    body = functools.partial(_pool_body, c_real=c_real, c_pad=c_pad, bb=bb, n_past=n_past, tok_w=tok_w,
                             n_alias=len(aliases))
    return pl.pallas_call(
        body,
        grid=(batch // bb, nch),
        in_specs=in_specs,
        out_specs=[row_spec, pl.BlockSpec((bb, POOL_HIST - 1, tok_w), lambda b, ch: (b, 0, 0))],
        out_shape=[jax.ShapeDtypeStruct(y_shape, BF16),
                   jax.ShapeDtypeStruct((batch, POOL_HIST - 1, tok_w), F32)],
        scratch_shapes=[pltpu.VMEM((bb, POOL_HIST + c_pad, tok_w), F32), pltpu.VMEM((rows, tok_w), F32)],
        input_output_aliases=aliases,
        compiler_params=_params("arbitrary", "arbitrary"),
        name="pool_mixer",
    )(*args)


def _softmax_rows(s):
    p = jnp.exp(s - jnp.max(s, axis=-1, keepdims=True))
    return p / jnp.sum(p, axis=-1, keepdims=True)


def _xattn_prompt_body(q_ref, k_ref, v_ref, y_prev_ref, o_ref):
    del y_prev_ref
    for h in range(N_MEM_HEADS):
        lo = h * HEAD_DIM
        s = _bdot_nt(q_ref[:, lo:lo + HEAD_DIM], k_ref[:, lo:lo + HEAD_DIM]) * (HEAD_DIM ** -0.5)
        o = _bdot(_softmax_rows(s), v_ref[:, lo:lo + HEAD_DIM])
        o_ref[:, lo:lo + HEAD_DIM] = o.astype(o_ref.dtype)


def cross_attn_prompt(q, q_col_block, kv, y_prev, *, batch, seq, tq):
    n_mem = kv.shape[1]
    tq = min(tq, seq)
    nq = seq // tq
    y_col_block = y_prev.shape[1] // MEM_W - 1
    return pl.pallas_call(
        _xattn_prompt_body,
        grid=(batch, nq),
        in_specs=[
            pl.BlockSpec((tq, MEM_W), lambda b, i: (b * nq + i, q_col_block)),
            pl.BlockSpec((None, n_mem, MEM_W), lambda b, i: (b, 0, 0)),
            pl.BlockSpec((None, n_mem, MEM_W), lambda b, i: (b, 0, 1)),
            pl.BlockSpec(memory_space=pl.ANY),
        ],
        out_specs=pl.BlockSpec((tq, MEM_W), lambda b, i: (b * nq + i, y_col_block)),
        out_shape=jax.ShapeDtypeStruct(y_prev.shape, y_prev.dtype),
        input_output_aliases={3: 0},
        compiler_params=_params("arbitrary", "arbitrary"),
        name="cross_attn_prompt",
    )(q, kv, kv, y_prev)


def _xattn_sample_body(q_ref, k_ref, v_ref, y_prev_ref, o_ref, q_scr, o_scr, *, bb, seq):
    del y_prev_ref
    n_rows = k_ref.shape[1]
    q_scr[...] = jnp.zeros(q_scr.shape, F32)
    for b in range(bb):
        q_scr[b, 0:seq, :] = q_ref[b * seq:(b + 1) * seq, :]
    shape = (N_MEM_HEADS * SUBLANE, n_rows)
    same_head = (lax.broadcasted_iota(jnp.int32, shape, 1) % N_MEM_HEADS
                 == lax.broadcasted_iota(jnp.int32, shape, 0) // SUBLANE)
    scores = []
    for b in range(bb):
        qb = q_scr[b]
        q_rows = jnp.concatenate([qb[:, h * HEAD_DIM:(h + 1) * HEAD_DIM] for h in range(N_MEM_HEADS)], axis=0)
        scores.append(_bdot_nt(q_rows, k_ref[b]) * (HEAD_DIM ** -0.5))
    probs = []
    for s in scores:
        p = jnp.exp(s - jnp.max(jnp.where(same_head, s, -jnp.inf), axis=-1, keepdims=True))
        p = jnp.where(same_head, p, 0.0)
        probs.append(p / jnp.sum(p, axis=-1, keepdims=True))
    outs = [_bdot(p, v_ref[b]) for b, p in enumerate(probs)]
    for b, o in enumerate(outs):
        for h in range(N_MEM_HEADS):
            o_scr[b * seq:(b + 1) * seq, h * HEAD_DIM:(h + 1) * HEAD_DIM] = o[h * SUBLANE:h * SUBLANE + seq]
    o_ref[...] = o_scr[...].astype(o_ref.dtype)


def cross_attn_sample(q, q_col_block, row_off, cache_k, cache_v, layer, y_prev, *, batch, seq, bb):
    n_rows = cache_k.shape[2]
    rows = bb * seq
    blk0 = row_off // rows
    y_col_block = y_prev.shape[1] // MEM_W - 1
    kv_spec = pl.BlockSpec((None, bb, n_rows, HEAD_DIM), lambda i: (layer, i, 0, 0))
    return pl.pallas_call(
        functools.partial(_xattn_sample_body, bb=bb, seq=seq),
        grid=(batch // bb,),
        in_specs=[pl.BlockSpec((rows, MEM_W), lambda i: (blk0 + i, q_col_block)), kv_spec, kv_spec,
                  pl.BlockSpec(memory_space=pl.ANY)],
        out_specs=pl.BlockSpec((rows, MEM_W), lambda i: (blk0 + i, y_col_block)),
        out_shape=jax.ShapeDtypeStruct(y_prev.shape, y_prev.dtype),
        scratch_shapes=[pltpu.VMEM((bb, SUBLANE, MEM_W), F32), pltpu.VMEM((rows, MEM_W), F32)],
        input_output_aliases={3: 0},
        compiler_params=_params("arbitrary"),
        name="cross_attn_sample",
    )(q, cache_k, cache_v, y_prev)


def kernel(x_prompt, x_sample, mem_prompt, state_delta_S, state_delta_conv, state_pool, cache_mem_k, cache_mem_v,
           norm_mix, norm_ffn, norm_mem, norm_final, w_in_delta, conv_w, a_log, dt_bias, delta_onorm,
           w_in_pool, w_pool_grp, pool_scale, w_mem_kv, w_out, w_gate_up, w_down):
    bp, lp, d = x_prompt.shape
    bs, ls, _ = x_sample.shape
    depth = w_out.shape[0]
    n_mem = mem_prompt.shape[1]
    n_delta, _, n_heads = state_delta_S.shape[:3]
    n_pool = state_pool.shape[0]
    tok_w = n_heads * HEAD_DIM
    main_w = 4 * tok_w
    d_ff = w_down.shape[1]
    tp, ts = bp * lp, bs * ls
    n_past = min(PAST_LEN, POOL_HIST - 1)
    y_shape = (tp + ts, tok_w + MEM_W)

    x = jnp.concatenate([x_prompt.reshape(tp, d), x_sample.reshape(ts, d)], axis=0)

    mem_rows = mem_prompt.reshape(bp * n_mem, d)
    p_kv = [matmul(rmsnorm(mem_rows, norm_mem[l], BF16), w_mem_kv, l, ncols=2 * MEM_W, tn=MEM_W, out_dtype=F32)
            for l in range(depth)]
    p_mem_k = jnp.stack([kv[:, :MEM_W].reshape(bp, n_mem, N_MEM_HEADS, HEAD_DIM) for kv in p_kv])
    p_mem_v = jnp.stack([kv[:, MEM_W:].reshape(bp, n_mem, N_MEM_HEADS, HEAD_DIM) for kv in p_kv])
    cache_k = cache_mem_k.reshape(depth, bs, n_mem * N_MEM_HEADS, HEAD_DIM)
    cache_v = cache_mem_v.reshape(depth, bs, n_mem * N_MEM_HEADS, HEAD_DIM)

    w_in_delta_t = jnp.swapaxes(w_in_delta, 1, 2)
    zero_s = jnp.zeros((n_delta, bp, n_heads, HEAD_DIM, HEAD_DIM), F32)
    zero_conv = jnp.zeros((1, bp, CONV_W - 1, 3 * tok_w), F32)
    zero_pool = jnp.zeros((1, bp, POOL_HIST - 1, tok_w), F32)

    p_conv, p_pool, s_conv, s_pool = [], [], [], []
    p_s = s_s = None
    di = pi = 0
    for l in range(depth):
        h = rmsnorm(x, norm_mix[l], BF16)
        if l % 2 == 0:
            proj = matmul(h, w_in_delta_t, di, ncols=main_w, tn=1024, out_dtype=F32, w_transposed=True)
            tail0 = main_w // LANE
            small = matmul(h, w_in_delta_t, di, ncols=MEM_W + LANE, tn=LANE, out_dtype=F32, w_transposed=True,
                           col_blocks=tuple(range(tail0, tail0 + (MEM_W + LANE) // LANE)),
                           out_slices=((2 * n_heads, 2 * n_heads + MEM_W), (0, LANE)))
            gates = (conv_w[di], a_log[di], dt_bias[di], delta_onorm[di])
            y, conv_p, p_s = delta_mixer(proj, small, 0, zero_conv, 0, zero_s, di, p_s, None, y_shape, *gates,
                                         batch=bp, seq=lp, chunk=64, bb=1, unit_group=n_heads)
            y, conv_s, s_s = delta_mixer(proj, small, tp, state_delta_conv, di, state_delta_S, di, s_s, y, y_shape,
                                         *gates, batch=bs, seq=ls, chunk=64, bb=4, unit_group=4 * n_heads)
            p_conv.append(conv_p); s_conv.append(conv_s)
            q_src, q_blk = small, 0
            di += 1
        else:
            proj = matmul(h, w_in_pool, pi, ncols=tok_w + MEM_W, tn=1024, out_dtype=F32)
            y, pool_p = pool_mixer(proj, 0, zero_pool, 0, w_pool_grp, pool_scale, pi, None, y_shape,
                                   batch=bp, seq=lp, chunk=128, bb=1, n_past=0)
            y, pool_s = pool_mixer(proj, tp, state_pool, pi, w_pool_grp, pool_scale, pi, y, y_shape,
                                   batch=bs, seq=ls, chunk=128, bb=16, n_past=n_past)
            p_pool.append(pool_p); s_pool.append(pool_s)
            q_src, q_blk = proj, tok_w // MEM_W
            pi += 1

        y = cross_attn_prompt(q_src, q_blk, p_kv[l].reshape(bp, n_mem, 2 * MEM_W), y, batch=bp, seq=lp, tq=512)
        y = cross_attn_sample(q_src, q_blk, tp, cache_k, cache_v, l, y, batch=bs, seq=ls, bb=8)

        x, h2 = matmul_residual_norm(y, w_out, l, x, norm_ffn[l],
                                     tm=next(c for c in (544, 512, 256, 128) if (tp + ts) % c == 0))
        act = matmul(h2, w_gate_up, l, ncols=d_ff, tn=512, out_dtype=BF16, col_blocks=(0, d_ff // 512), swiglu=True)
        x = matmul(act, w_down, l, ncols=d, tn=512, out_dtype=F32, residual=x, tm_max=512)

    y_p, y_s = rmsnorm_split(x, norm_final, tp)
    return (y_p.reshape(bp, lp, d), y_s.reshape(bs, ls, d),
            p_s, jnp.stack(p_conv), jnp.stack(p_pool), p_mem_k, p_mem_v,
            s_s, jnp.stack(s_conv), jnp.stack(s_pool))
```

```python
import functools
import math

import jax
import jax.numpy as jnp
from jax import lax
from jax.experimental import pallas as pl
from jax.experimental.pallas import tpu as pltpu

F32 = jnp.float32
BF16 = jnp.bfloat16
HIGHEST = lax.Precision.HIGHEST

EPS = 1e-6
HEAD_DIM = 128
N_MEM_HEADS = 4
MEM_W = N_MEM_HEADS * HEAD_DIM
CONV_W = 4
POOL_WINDOWS = (2, 4, 8, 16)
POOL_HIST = max(POOL_WINDOWS)
PAST_LEN = 16384
LANE = 128
SUBLANE = 8
MXU_COLS = 256
CARRY_OFF = SUBLANE - (CONV_W - 1)
VMEM_LIMIT = 56 * 1024 * 1024


def _params(*sem):
    return pltpu.CompilerParams(dimension_semantics=sem, vmem_limit_bytes=VMEM_LIMIT)


def _sigmoid(x):
    return 1.0 / (1.0 + jnp.exp(-x))


def _silu(x):
    half = 0.5 * x
    return half + half * jnp.tanh(half)


def _dot(a, b, precision=None):
    return jnp.dot(a, b, preferred_element_type=F32, precision=precision)


def _dot_nt(a, b, precision=None):
    return lax.dot_general(a, b, (((1,), (1,)), ((), ())), preferred_element_type=F32, precision=precision)


def _dot_tn(a, b, precision=None):
    return lax.dot_general(a, b, (((0,), (0,)), ((), ())), preferred_element_type=F32, precision=precision)


def _bdot(a, b):
    return _dot(a.astype(BF16), b.astype(BF16))


def _bdot_nt(a, b):
    return _dot_nt(a.astype(BF16), b.astype(BF16))


def _bdot_tn(a, b):
    return _dot_tn(a.astype(BF16), b.astype(BF16))


def _alias_args(in_specs, args, prev, out_index):
    if prev is None:
        return {}
    in_specs.append(pl.BlockSpec(memory_space=pl.ANY))
    args.append(prev)
    return {len(args) - 1: out_index}


def _rmsnorm_body(x_ref, w_ref, o_ref):
    x = x_ref[...]
    ms = jnp.mean(x * x, axis=-1, keepdims=True)
    o_ref[...] = (x * lax.rsqrt(ms + EPS) * w_ref[...]).astype(o_ref.dtype)


def rmsnorm(x, w, out_dtype):
    t, d = x.shape
    tm = next(c for c in (512, 256, 128) if t % c == 0)
    return pl.pallas_call(
        _rmsnorm_body,
        grid=(t // tm,),
        in_specs=[pl.BlockSpec((tm, d), lambda i: (i, 0)), pl.BlockSpec((1, d), lambda i: (0, 0))],
        out_specs=pl.BlockSpec((tm, d), lambda i: (i, 0)),
        out_shape=jax.ShapeDtypeStruct((t, d), out_dtype),
        compiler_params=_params("arbitrary"),
        name="rmsnorm",
    )(x, w.reshape(1, d))


def _rmsnorm_concat_body(a_ref, b_ref, w_ref, o_ref, *, n_a):
    x = jnp.where(pl.program_id(0) < n_a, a_ref[...], b_ref[...])
    ms = jnp.mean(x * x, axis=-1, keepdims=True)
    o_ref[...] = (x * lax.rsqrt(ms + EPS) * w_ref[...]).astype(o_ref.dtype)


def rmsnorm_concat(xa, xb, w, out_dtype):
    (ra, d), rb = xa.shape, xb.shape[0]
    tm = next(c for c in (512, 256, 128) if ra % c == 0 and rb % c == 0)
    n_a = ra // tm
    return pl.pallas_call(
        functools.partial(_rmsnorm_concat_body, n_a=n_a),
        grid=((ra + rb) // tm,),
        in_specs=[pl.BlockSpec((tm, d), lambda i: (jnp.minimum(i, n_a - 1), 0)),
                  pl.BlockSpec((tm, d), lambda i: (jnp.maximum(i - n_a, 0), 0)),
                  pl.BlockSpec((1, d), lambda i: (0, 0))],
        out_specs=pl.BlockSpec((tm, d), lambda i: (i, 0)),
        out_shape=jax.ShapeDtypeStruct((ra + rb, d), out_dtype),
        compiler_params=_params("arbitrary"),
        name="rmsnorm_concat",
    )(xa, xb, w.reshape(1, d))


def _rmsnorm_split_body(x_ref, w_ref, a_ref, b_ref, *, n_a):
    x = x_ref[...]
    ms = jnp.mean(x * x, axis=-1, keepdims=True)
    y = x * lax.rsqrt(ms + EPS) * w_ref[...]

    @pl.when(pl.program_id(0) < n_a)
    def _first():
        a_ref[...] = y

    @pl.when(pl.program_id(0) >= n_a)
    def _second():
        b_ref[...] = y


def rmsnorm_split(x, w, rows_a):
    t, d = x.shape
    tm = next(c for c in (512, 256, 128) if rows_a % c == 0 and (t - rows_a) % c == 0)
    n_a = rows_a // tm
    return pl.pallas_call(
        functools.partial(_rmsnorm_split_body, n_a=n_a),
        grid=(t // tm,),
        in_specs=[pl.BlockSpec((tm, d), lambda i: (i, 0)), pl.BlockSpec((1, d), lambda i: (0, 0))],
        out_specs=[pl.BlockSpec((tm, d), lambda i: (jnp.minimum(i, n_a - 1), 0)),
                   pl.BlockSpec((tm, d), lambda i: (jnp.maximum(i - n_a, 0), 0))],
        out_shape=[jax.ShapeDtypeStruct((rows_a, d), F32), jax.ShapeDtypeStruct((t - rows_a, d), F32)],
        compiler_params=_params("arbitrary"),
        name="rmsnorm_split",
    )(x, w.reshape(1, d))


def _mm_body(*refs, n_w, tn, has_res, swiglu, out_slices, slab_valid, w_transposed):
    x_ref = refs[0]
    w_refs = refs[1:1 + n_w]
    res_ref = refs[1 + n_w] if has_res else None
    o_ref = refs[-2]
    wb_ref = refs[-1]

    @pl.when(pl.program_id(1) == 0)
    def _cast_weights():
        for i, w_ref in enumerate(w_refs):
            valid = slab_valid[i]
            if w_transposed:
                wb_ref[i * tn:i * tn + valid, :] = w_ref[0:valid, :].astype(BF16)
                if valid < tn:
                    wb_ref[i * tn + valid:(i + 1) * tn, :] = jnp.zeros((tn - valid, wb_ref.shape[1]), BF16)
            else:
                wb_ref[:, i * tn:i * tn + valid] = w_ref[:, 0:valid].astype(BF16)
                if valid < tn:
                    wb_ref[:, i * tn + valid:(i + 1) * tn] = jnp.zeros((wb_ref.shape[0], tn - valid), BF16)

    def product(lo, hi):
        if w_transposed:
            return _dot_nt(x_ref[...], wb_ref[lo:hi, :])
        return _dot(x_ref[...], wb_ref[:, lo:hi])

    if out_slices is not None:
        acc = product(0, n_w * tn)
        acc = jnp.concatenate([acc[:, lo:hi] for lo, hi in out_slices], axis=1)
        o_ref[...] = acc.astype(o_ref.dtype)
        return
    for c0 in range(0, tn, MXU_COLS):
        c1 = min(c0 + MXU_COLS, tn)
        acc = product(c0, c1)
        if swiglu:
            acc = _silu(acc) * product(tn + c0, tn + c1)
        if has_res:
            acc = acc + res_ref[:, c0:c1]
        o_ref[:, c0:c1] = acc.astype(o_ref.dtype)


def matmul(x, w, layer, *, ncols, tn, out_dtype, col_blocks=(0,), residual=None, swiglu=False, out_slices=None,
           w_transposed=False, tm_max=1088):
    t, k = x.shape
    tm = next(c for c in (2176, 1088, 1024, 512, 256, 128) if c <= tm_max and t % c == 0)
    n_w = len(col_blocks)
    to = tn if out_slices is None else sum(hi - lo for lo, hi in out_slices)
    in_specs = [pl.BlockSpec((tm, k), lambda j, m: (m, 0))]
    for cb in col_blocks:
        if w_transposed:
            in_specs.append(pl.BlockSpec((None, tn, k), lambda j, m, cb=cb: (layer, cb + j, 0)))
        else:
            in_specs.append(pl.BlockSpec((None, k, tn), lambda j, m, cb=cb: (layer, 0, cb + j)))
    args = [x] + [w] * n_w
    if residual is not None:
        in_specs.append(pl.BlockSpec((tm, to), lambda j, m: (m, j)))
        args.append(residual)
    n_tiles = ncols // to
    n_total = w.shape[1] if w_transposed else w.shape[2]
    slab_valid = tuple(min(tn, n_total - (cb + n_tiles - 1) * tn) for cb in col_blocks)
    assert all(v == tn for v in slab_valid) or n_tiles == 1
    body = functools.partial(_mm_body, n_w=n_w, tn=tn, has_res=residual is not None, swiglu=swiglu,
                             out_slices=out_slices, slab_valid=slab_valid, w_transposed=w_transposed)
    return pl.pallas_call(
        body,
        grid=(n_tiles, t // tm),
        in_specs=in_specs,
        out_specs=pl.BlockSpec((tm, to), lambda j, m: (m, j)),
        out_shape=jax.ShapeDtypeStruct((t, ncols), out_dtype),
        scratch_shapes=[pltpu.VMEM((n_w * tn, k) if w_transposed else (k, n_w * tn), BF16)],
        compiler_params=_params("arbitrary", "arbitrary"),
        name="matmul",
    )(*args)


def _proj_norm_body(*refs, n_a):
    x_ref, w_ref, res_ref = refs[:3]
    res_b_ref = refs[3] if n_a is not None else None
    nw_ref, o_ref, h_ref, wb_ref = refs[-4:]

    def residual(c0):
        res = res_ref[:, c0:c0 + MXU_COLS]
        if res_b_ref is None:
            return res
        return jnp.where(pl.program_id(0) < n_a, res, res_b_ref[:, c0:c0 + MXU_COLS])

    @pl.when(pl.program_id(0) == 0)
    def _cast_weights():
        wb_ref[...] = w_ref[...].astype(BF16)

    n = o_ref.shape[1]
    sumsq = jnp.zeros((o_ref.shape[0], LANE), F32)
    for c0 in range(0, n, MXU_COLS):
        acc = _dot(x_ref[...], wb_ref[:, c0:c0 + MXU_COLS]) + residual(c0)
        o_ref[:, c0:c0 + MXU_COLS] = acc
        sq = acc * acc
        for l0 in range(0, MXU_COLS, LANE):
            sumsq = sumsq + sq[:, l0:l0 + LANE]
    scale = lax.rsqrt(jnp.sum(sumsq, axis=-1, keepdims=True) / n + EPS)
    for c0 in range(0, n, MXU_COLS):
        h = o_ref[:, c0:c0 + MXU_COLS] * scale * nw_ref[:, c0:c0 + MXU_COLS]
        h_ref[:, c0:c0 + MXU_COLS] = h.astype(h_ref.dtype)


def matmul_residual_norm(x, w, layer, residual, norm_w, *, tm):
    t, k = x.shape
    n = w.shape[2]
    in_specs = [
        pl.BlockSpec((tm, k), lambda m: (m, 0)),
        pl.BlockSpec((None, k, n), lambda m: (layer, 0, 0), pipeline_mode=pl.Buffered(1)),
    ]
    if isinstance(residual, tuple):
        n_a = residual[0].shape[0] // tm
        in_specs += [pl.BlockSpec((tm, n), lambda m: (jnp.minimum(m, n_a - 1), 0)),
                     pl.BlockSpec((tm, n), lambda m: (jnp.maximum(m - n_a, 0), 0))]
    else:
        n_a, residual = None, (residual,)
        in_specs.append(pl.BlockSpec((tm, n), lambda m: (m, 0)))
    in_specs.append(pl.BlockSpec((1, n), lambda m: (0, 0)))
    return pl.pallas_call(
        functools.partial(_proj_norm_body, n_a=n_a),
        grid=(t // tm,),
        in_specs=in_specs,
        out_specs=[pl.BlockSpec((tm, n), lambda m: (m, 0)), pl.BlockSpec((tm, n), lambda m: (m, 0))],
        out_shape=[jax.ShapeDtypeStruct((t, n), F32), jax.ShapeDtypeStruct((t, n), BF16)],
        scratch_shapes=[pltpu.VMEM((k, n), BF16)],
        compiler_params=_params("arbitrary"),
        name="matmul_residual_norm",
    )(x, w, *residual, norm_w.reshape(1, n))


def _delta_body(*refs, c_real, c_pad, bb, n_heads, tok_w, unit_group, n_alias, seq_minor):
    main_ref, small_ref, cbuf_ref, s0_ref, cw_ref, alog_ref, dtb_ref, onw_ref = refs[:8]
    y_ref, cnew_ref, s_ref, xp_scr, ab_scr, y_scr = refs[8 + n_alias:]
    qkv_w = 3 * tok_w
    c = c_pad
    padded = c_pad > c_real

    @pl.when(pl.program_id(1) == 0)
    def _init():
        s_ref[...] = s0_ref[...]
        for s in range(bb):
            xp_scr[s, 0:CARRY_OFF, :] = jnp.zeros((CARRY_OFF, xp_scr.shape[2]), F32)
            if seq_minor:
                for i in range(CONV_W - 1):
                    xp_scr[s, CARRY_OFF + i:CARRY_OFF + i + 1, 0:qkv_w] = cbuf_ref[i, s:s + 1, :]
            else:
                xp_scr[s, CARRY_OFF:SUBLANE, 0:qkv_w] = cbuf_ref[s]
            if padded:
                xp_scr[s, SUBLANE + c_real:SUBLANE + c, :] = jnp.zeros((c - c_real, xp_scr.shape[2]), F32)
        if padded:
            ab_scr[...] = jnp.zeros(ab_scr.shape, F32)

    for s in range(bb):
        xp_scr[s, SUBLANE:SUBLANE + c_real, :] = main_ref[s * c_real:(s + 1) * c_real, :]
        ab_scr[s, 0:c_real, :] = small_ref[s * c_real:(s + 1) * c_real, MEM_W:MEM_W + LANE]

    rows = lax.broadcasted_iota(jnp.int32, (c, c), 0)
    cols = lax.broadcasted_iota(jnp.int32, (c, c), 1)
    incl = rows >= cols
    strict = rows > cols
    eye_l = (lax.broadcasted_iota(jnp.int32, (LANE, LANE), 0)
             == lax.broadcasted_iota(jnp.int32, (LANE, LANE), 1)).astype(F32)

    beta_all, gc_all, gc_t, gam_all, kdec_all, gend_all = [], [], [], [], [], []
    for s in range(bb):
        ab = ab_scr[s]
        a_in = ab + dtb_ref[...]
        softplus = jnp.maximum(a_in, 0.0) + jnp.log(1.0 + jnp.exp(-jnp.abs(a_in)))
        g = -jnp.exp(alog_ref[...]) * softplus
        beta = _sigmoid(ab)
        if padded:
            live = lax.broadcasted_iota(jnp.int32, (c, LANE), 0) < c_real
            g = jnp.where(live, g, 0.0)
            beta = jnp.where(live, beta, 0.0)
        gc = _dot(incl.astype(F32), g, HIGHEST)
        g_last = gc[c - 1:c, :]
        beta_all.append(beta)
        gc_all.append(gc)
        gc_t.append(_dot_nt(eye_l, gc, HIGHEST))
        gam_all.append(jnp.exp(gc))
        kdec_all.append(jnp.exp(g_last - gc))
        gend_all.append(jnp.exp(g_last))

    def conv_slab(s, col):
        blk = xp_scr[s, :, col:col + LANE]
        n = SUBLANE + c
        acc = blk[SUBLANE:] * cw_ref[CONV_W - 1:CONV_W, col:col + LANE]
        for i in range(CONV_W - 1):
            tap = pltpu.roll(blk, n - (CARRY_OFF + i), axis=0)[:c]
            acc = acc + tap * cw_ref[i:i + 1, col:col + LANE]
        return _silu(acc)

    n_levels = int(math.log2(c))
    all_units = [(s, h) for s in range(bb) for h in range(n_heads)]
    for g0 in range(0, len(all_units), unit_group):
        units = all_units[g0:g0 + unit_group]
        q, k, v, beta, gam, decay = {}, {}, {}, {}, {}, {}
        for un in units:
            s, h = un
            lo = h * HEAD_DIM
            qh = conv_slab(s, lo)
            kh = conv_slab(s, tok_w + lo)
            v[un] = conv_slab(s, 2 * tok_w + lo)
            q[un] = qh * lax.rsqrt(jnp.sum(qh * qh, axis=-1, keepdims=True) + EPS) * (HEAD_DIM ** -0.5)
            k[un] = kh * lax.rsqrt(jnp.sum(kh * kh, axis=-1, keepdims=True) + EPS)
            beta[un] = beta_all[s][:, n_heads + h:n_heads + h + 1]
            gam[un] = gam_all[s][:, h:h + 1]
            diff = gc_all[s][:, h:h + 1] - gc_t[s][h:h + 1, :]
            decay[un] = jnp.where(incl, jnp.exp(jnp.where(incl, diff, 0.0)), 0.0)

        k_b = {un: k[un].astype(BF16) for un in units}
        qk_kk = {un: _dot_nt(jnp.concatenate([q[un].astype(BF16), k_b[un]], axis=0), k_b[un]) for un in units}

        x_pow = {un: -(jnp.where(strict, qk_kk[un][c:] * decay[un], 0.0) * beta[un]) for un in units}
        t_off = dict(x_pow)
        for lvl in range(1, n_levels + 1):
            for un in units:
                x_b = x_pow[un].astype(BF16)
                if lvl == 1:
                    x_pow[un] = _dot(x_b, x_b)
                elif lvl < n_levels:
                    both = _dot(jnp.concatenate([t_off[un].astype(BF16), x_b], axis=0), x_b)
                    t_off[un] = t_off[un] + x_pow[un] + both[:c]
                    x_pow[un] = both[c:]
                else:
                    t_off[un] = t_off[un] + x_pow[un] + _dot(t_off[un].astype(BF16), x_b)

        sol = {}
        for un in units:
            rhs = jnp.concatenate([v[un] * beta[un], k[un] * (beta[un] * gam[un])], axis=1)
            sol[un] = rhs + _bdot(t_off[un], rhs)

        s_old = {un: s_ref[un[0], un[1]] for un in units}
        from_state = {un: _bdot(jnp.concatenate([sol[un][:, HEAD_DIM:], q[un] * gam[un]], axis=0), s_old[un])
                      for un in units}
        u = {un: sol[un][:, :HEAD_DIM] - from_state[un][:c] for un in units}
        o = {un: from_state[un][c:] + _bdot(jnp.where(incl, qk_kk[un][:c] * decay[un], 0.0), u[un])
             for un in units}
        s_new = {un: (s_old[un] * gend_all[un[0]][:, un[1]:un[1] + 1]
                      + _bdot_tn(k[un] * kdec_all[un[0]][:, un[1]:un[1] + 1], u[un])) for un in units}
        for un in units:
            s_ref[un[0], un[1]] = s_new[un]

        for un in units:
            s, h = un
            lo = h * HEAD_DIM
            oh = o[un] * lax.rsqrt(jnp.mean(o[un] * o[un], axis=-1, keepdims=True) + EPS) * onw_ref[...]
            z = xp_scr[s, SUBLANE:SUBLANE + c, qkv_w + lo:qkv_w + lo + HEAD_DIM]
            out = oh * _silu(z)
            if padded:
                y_scr[s * c_real:(s + 1) * c_real, lo:lo + HEAD_DIM] = out[:c_real]
            else:
                y_ref[:, lo:lo + HEAD_DIM] = out.astype(y_ref.dtype)

    if padded:
        y_ref[...] = y_scr[...].astype(y_ref.dtype)
    for s in range(bb):
        carry = xp_scr[s, CARRY_OFF + c_real:SUBLANE + c_real, 0:qkv_w]
        xp_scr[s, CARRY_OFF:SUBLANE, 0:qkv_w] = carry
        if seq_minor:
            for i in range(CONV_W - 1):
                cnew_ref[i, s:s + 1, :] = carry[i:i + 1, :]
        else:
            cnew_ref[s] = carry


def delta_mixer(main, small, row_off, conv_buf, conv_layer, conv_out_prev, s0, s_layer, s_out_prev, y_prev, y_shape,
                conv_w, a_log, dt_bias, onorm_w, *, batch, seq, chunk, bb, unit_group, seq_minor):
    n_layers, _, n_heads = s0.shape[:3]
    tok_w = n_heads * HEAD_DIM
    main_w = 4 * tok_w
    qkv_w = 3 * tok_w
    c_real = min(chunk, seq)
    c_pad = max(c_real, SUBLANE)
    assert bb == 1 or c_real == seq
    nch = seq // c_real
    rows = bb * c_real
    blk0 = row_off // rows
    row_spec = lambda w: pl.BlockSpec((rows, w), lambda b, ch: (blk0 + b * nch + ch, 0))
    pad_lane = lambda p: jnp.zeros((1, LANE), F32).at[0, :n_heads].set(p.astype(F32))
    s_spec = pl.BlockSpec((None, bb, n_heads, HEAD_DIM, HEAD_DIM), lambda b, ch: (s_layer, b, 0, 0, 0))
    if seq_minor:
        conv_in_spec = pl.BlockSpec((None, CONV_W - 1, bb, qkv_w), lambda b, ch: (conv_layer, 0, b, 0))
        conv_out_spec = conv_in_spec
        conv_out_shape = (conv_buf.shape[0], CONV_W - 1, batch, qkv_w)
    else:
        conv_in_spec = pl.BlockSpec((None, bb, CONV_W - 1, qkv_w), lambda b, ch: (conv_layer, b, 0, 0))
        conv_out_spec = pl.BlockSpec((bb, CONV_W - 1, qkv_w), lambda b, ch: (b, 0, 0))
        conv_out_shape = (batch, CONV_W - 1, qkv_w)
    in_specs = [
        row_spec(main_w),
        row_spec(small.shape[-1]),
        conv_in_spec,
        s_spec,
        pl.BlockSpec((CONV_W, qkv_w), lambda b, ch: (0, 0)),
        pl.BlockSpec((1, LANE), lambda b, ch: (0, 0)),
        pl.BlockSpec((1, LANE), lambda b, ch: (0, 0)),
        pl.BlockSpec((1, HEAD_DIM), lambda b, ch: (0, 0)),
    ]
    args = [main, small, conv_buf, s0, conv_w, pad_lane(a_log), pad_lane(dt_bias), onorm_w.reshape(1, HEAD_DIM)]
    aliases = {}
    aliases.update(_alias_args(in_specs, args, y_prev, 0))
    aliases.update(_alias_args(in_specs, args, conv_out_prev, 1))
    aliases.update(_alias_args(in_specs, args, s_out_prev, 2))
    body = functools.partial(_delta_body, c_real=c_real, c_pad=c_pad, bb=bb, n_heads=n_heads, tok_w=tok_w,
                             unit_group=unit_group, n_alias=len(aliases), seq_minor=seq_minor)
    return pl.pallas_call(
        body,
        grid=(batch // bb, nch),
        in_specs=in_specs,
        out_specs=[
            row_spec(tok_w),
            conv_out_spec,
            s_spec,
        ],
        out_shape=[
            jax.ShapeDtypeStruct(y_shape, BF16),
            jax.ShapeDtypeStruct(conv_out_shape, F32),
            jax.ShapeDtypeStruct((n_layers, batch, n_heads, HEAD_DIM, HEAD_DIM), F32),
        ],
        scratch_shapes=[pltpu.VMEM((bb, SUBLANE + c_pad, main_w), F32), pltpu.VMEM((bb, c_pad, LANE), F32),
                        pltpu.VMEM((rows, tok_w), F32)],
        input_output_aliases=aliases,
        compiler_params=_params("arbitrary", "arbitrary"),
        name="delta_mixer",
    )(*args)


def _pool_body(*refs, c_real, c_pad, bb, n_past, tok_w, n_alias, seq_minor):
    u_ref, buf_ref, wg_ref, scale_ref = refs[:4]
    y_ref, pnew_ref, hist_scr, y_scr = refs[4 + n_alias:]
    c = c_pad
    padded = c_pad > c_real
    gw = tok_w // len(POOL_WINDOWS)

    @pl.when(pl.program_id(1) == 0)
    def _init():
        for s in range(bb):
            hist_scr[s, 0:1, :] = jnp.zeros((1, tok_w), F32)
            if seq_minor:
                for j in range(POOL_HIST - 1):
                    hist_scr[s, 1 + j:2 + j, :] = buf_ref[j, s:s + 1, :]
            else:
                hist_scr[s, 1:POOL_HIST, :] = buf_ref[s]
            if padded:
                hist_scr[s, POOL_HIST + c_real:POOL_HIST + c, :] = jnp.zeros((c - c_real, tok_w), F32)

    for s in range(bb):
        hist_scr[s, POOL_HIST:POOL_HIST + c_real, :] = u_ref[s * c_real:(s + 1) * c_real, :]
    pos = pl.program_id(1) * c_real + lax.broadcasted_iota(jnp.int32, (c, 1), 0)
    for gi, win in enumerate(POOL_WINDOWS):
        lo = gi * gw
        inv_cnt = 1.0 / jnp.minimum(win, pos + 1 + n_past).astype(F32)
        ds = []
        for s in range(bb):
            tok = hist_scr[s, POOL_HIST:POOL_HIST + c, lo:lo + gw]
            acc = tok
            for back in range(1, win):
                acc = acc + hist_scr[s, POOL_HIST - back:POOL_HIST - back + c, lo:lo + gw]
            ds.append(acc * inv_cnt - tok)
        d_all = ds[0] if bb == 1 else jnp.concatenate(ds, axis=0)
        y_all = _bdot(d_all, wg_ref[gi]) * scale_ref[:, lo:lo + gw]
        if padded:
            for s in range(bb):
                y_scr[s * c_real:(s + 1) * c_real, lo:lo + gw] = y_all[s * c:s * c + c_real]
        else:
            y_ref[:, lo:lo + gw] = y_all.astype(y_ref.dtype)

    if padded:
        y_ref[...] = y_scr[...].astype(y_ref.dtype)
    for s in range(bb):
        tail = hist_scr[s, c_real + 1:c_real + POOL_HIST, :]
        hist_scr[s, 1:POOL_HIST, :] = tail
        if seq_minor:
            for j in range(POOL_HIST - 1):
                pnew_ref[j, s:s + 1, :] = tail[j:j + 1, :]
        else:
            pnew_ref[s] = tail


def pool_mixer(u, row_off, buf, buf_layer, buf_out_prev, w_grp, scale, layer, y_prev, y_shape, *,
               batch, seq, chunk, bb, n_past, seq_minor):
    tok_w = buf.shape[-1]
    gw = tok_w // len(POOL_WINDOWS)
    c_real = min(chunk, seq)
    c_pad = max(c_real, SUBLANE)
    assert bb == 1 or c_real == seq
    nch = seq // c_real
    rows = bb * c_real
    blk0 = row_off // rows
    row_spec = pl.BlockSpec((rows, tok_w), lambda b, ch: (blk0 + b * nch + ch, 0))
    if seq_minor:
        buf_in_spec = pl.BlockSpec((None, POOL_HIST - 1, bb, tok_w), lambda b, ch: (buf_layer, 0, b, 0))
        buf_out_spec = buf_in_spec
        buf_out_shape = (buf.shape[0], POOL_HIST - 1, batch, tok_w)
    else:
        buf_in_spec = pl.BlockSpec((None, bb, POOL_HIST - 1, tok_w), lambda b, ch: (buf_layer, b, 0, 0))
        buf_out_spec = pl.BlockSpec((bb, POOL_HIST - 1, tok_w), lambda b, ch: (b, 0, 0))
        buf_out_shape = (batch, POOL_HIST - 1, tok_w)
    in_specs = [
        row_spec,
        buf_in_spec,
        pl.BlockSpec((None, len(POOL_WINDOWS), gw, gw), lambda b, ch: (layer, 0, 0, 0)),
        pl.BlockSpec((None, 1, tok_w), lambda b, ch: (layer, 0, 0)),
    ]
    args = [u, buf, w_grp, scale.reshape(scale.shape[0], 1, tok_w)]
    aliases = _alias_args(in_specs, args, y_prev, 0)
    aliases.update(_alias_args(in_specs, args, buf_out_prev, 1))
    body = functools.partial(_pool_body, c_real=c_real, c_pad=c_pad, bb=bb, n_past=n_past, tok_w=tok_w,
                             n_alias=len(aliases), seq_minor=seq_minor)
    return pl.pallas_call(
        body,
        grid=(batch // bb, nch),
        in_specs=in_specs,
        out_specs=[row_spec, buf_out_spec],
        out_shape=[jax.ShapeDtypeStruct(y_shape, BF16), jax.ShapeDtypeStruct(buf_out_shape, F32)],
        scratch_shapes=[pltpu.VMEM((bb, POOL_HIST + c_pad, tok_w), F32), pltpu.VMEM((rows, tok_w), F32)],
        input_output_aliases=aliases,
        compiler_params=_params("arbitrary", "arbitrary"),
        name="pool_mixer",
    )(*args)


def _softmax_rows(s):
    p = jnp.exp(s - jnp.max(s, axis=-1, keepdims=True))
    return p / jnp.sum(p, axis=-1, keepdims=True)


def _xattn_prompt_body(q_ref, k_ref, v_ref, y_prev_ref, o_ref):
    del y_prev_ref
    for h in range(N_MEM_HEADS):
        lo = h * HEAD_DIM
        s = _bdot_nt(q_ref[:, lo:lo + HEAD_DIM], k_ref[:, lo:lo + HEAD_DIM]) * (HEAD_DIM ** -0.5)
        o = _bdot(_softmax_rows(s), v_ref[:, lo:lo + HEAD_DIM])
        o_ref[:, lo:lo + HEAD_DIM] = o.astype(o_ref.dtype)


def cross_attn_prompt(q, q_col_block, kv, y_prev, *, batch, seq, tq):
    n_mem = kv.shape[1]
    tq = min(tq, seq)
    nq = seq // tq
    y_col_block = y_prev.shape[1] // MEM_W - 1
    return pl.pallas_call(
        _xattn_prompt_body,
        grid=(batch, nq),
        in_specs=[
            pl.BlockSpec((tq, MEM_W), lambda b, i: (b * nq + i, q_col_block)),
            pl.BlockSpec((None, n_mem, MEM_W), lambda b, i: (b, 0, 0)),
            pl.BlockSpec((None, n_mem, MEM_W), lambda b, i: (b, 0, 1)),
            pl.BlockSpec(memory_space=pl.ANY),
        ],
        out_specs=pl.BlockSpec((tq, MEM_W), lambda b, i: (b * nq + i, y_col_block)),
        out_shape=jax.ShapeDtypeStruct(y_prev.shape, y_prev.dtype),
        input_output_aliases={3: 0},
        compiler_params=_params("arbitrary", "arbitrary"),
        name="cross_attn_prompt",
    )(q, kv, kv, y_prev)


def _xattn_sample_body(q_ref, k_ref, v_ref, y_prev_ref, o_ref, q_scr, o_scr, *, bb, seq):
    del y_prev_ref
    n_rows = k_ref.shape[1]
    q_scr[...] = jnp.zeros(q_scr.shape, F32)
    for b in range(bb):
        q_scr[b, 0:seq, :] = q_ref[b * seq:(b + 1) * seq, :]
    shape = (N_MEM_HEADS * SUBLANE, n_rows)
    same_head = (lax.broadcasted_iota(jnp.int32, shape, 1) % N_MEM_HEADS
                 == lax.broadcasted_iota(jnp.int32, shape, 0) // SUBLANE)
    scores = []
    for b in range(bb):
        qb = q_scr[b]
        q_rows = jnp.concatenate([qb[:, h * HEAD_DIM:(h + 1) * HEAD_DIM] for h in range(N_MEM_HEADS)], axis=0)
        scores.append(_bdot_nt(q_rows, k_ref[b]) * (HEAD_DIM ** -0.5))
    probs = []
    for s in scores:
        p = jnp.exp(s - jnp.max(jnp.where(same_head, s, -jnp.inf), axis=-1, keepdims=True))
        p = jnp.where(same_head, p, 0.0)
        probs.append(p / jnp.sum(p, axis=-1, keepdims=True))
    outs = [_bdot(p, v_ref[b]) for b, p in enumerate(probs)]
    for b, o in enumerate(outs):
        for h in range(N_MEM_HEADS):
            o_scr[b * seq:(b + 1) * seq, h * HEAD_DIM:(h + 1) * HEAD_DIM] = o[h * SUBLANE:h * SUBLANE + seq]
    o_ref[...] = o_scr[...].astype(o_ref.dtype)


def cross_attn_sample(q, q_col_block, row_off, cache_k, cache_v, layer, y_prev, *, batch, seq, bb):
    n_rows = cache_k.shape[2]
    rows = bb * seq
    blk0 = row_off // rows
    y_col_block = y_prev.shape[1] // MEM_W - 1
    kv_spec = pl.BlockSpec((None, bb, n_rows, HEAD_DIM), lambda i: (layer, i, 0, 0))
    return pl.pallas_call(
        functools.partial(_xattn_sample_body, bb=bb, seq=seq),
        grid=(batch // bb,),
        in_specs=[pl.BlockSpec((rows, MEM_W), lambda i: (blk0 + i, q_col_block)), kv_spec, kv_spec,
                  pl.BlockSpec(memory_space=pl.ANY)],
        out_specs=pl.BlockSpec((rows, MEM_W), lambda i: (blk0 + i, y_col_block)),
        out_shape=jax.ShapeDtypeStruct(y_prev.shape, y_prev.dtype),
        scratch_shapes=[pltpu.VMEM((bb, SUBLANE, MEM_W), F32), pltpu.VMEM((rows, MEM_W), F32)],
        input_output_aliases={3: 0},
        compiler_params=_params("arbitrary"),
        name="cross_attn_sample",
    )(q, cache_k, cache_v, y_prev)


def kernel(x_prompt, x_sample, mem_prompt, state_delta_S, state_delta_conv, state_pool, cache_mem_k, cache_mem_v,
           norm_mix, norm_ffn, norm_mem, norm_final, w_in_delta, conv_w, a_log, dt_bias, delta_onorm,
           w_in_pool, w_pool_grp, pool_scale, w_mem_kv, w_out, w_gate_up, w_down):
    bp, lp, d = x_prompt.shape
    bs, ls, _ = x_sample.shape
    depth = w_out.shape[0]
    n_mem = mem_prompt.shape[1]
    n_delta, _, n_heads = state_delta_S.shape[:3]
    tok_w = n_heads * HEAD_DIM
    main_w = 4 * tok_w
    d_ff = w_down.shape[1]
    tp, ts = bp * lp, bs * ls
    n_past = min(PAST_LEN, POOL_HIST - 1)
    y_shape = (tp + ts, tok_w + MEM_W)

    x = (x_prompt.reshape(tp, d), x_sample.reshape(ts, d))

    mem_rows = mem_prompt.reshape(bp * n_mem, d)
    p_kv = [matmul(rmsnorm(mem_rows, norm_mem[l], BF16), w_mem_kv, l, ncols=2 * MEM_W, tn=MEM_W, out_dtype=F32)
            for l in range(depth)]
    p_mem_k = jnp.stack([kv[:, :MEM_W].reshape(bp, n_mem, N_MEM_HEADS, HEAD_DIM) for kv in p_kv])
    p_mem_v = jnp.stack([kv[:, MEM_W:].reshape(bp, n_mem, N_MEM_HEADS, HEAD_DIM) for kv in p_kv])
    cache_k = cache_mem_k.reshape(depth, bs, n_mem * N_MEM_HEADS, HEAD_DIM)
    cache_v = cache_mem_v.reshape(depth, bs, n_mem * N_MEM_HEADS, HEAD_DIM)

    w_in_delta_t = jnp.swapaxes(w_in_delta, 1, 2)
    zero_s = jnp.zeros((n_delta, bp, n_heads, HEAD_DIM, HEAD_DIM), F32)
    zero_conv = jnp.zeros((1, bp, CONV_W - 1, 3 * tok_w), F32)
    zero_pool = jnp.zeros((1, bp, POOL_HIST - 1, tok_w), F32)
    conv_in = jnp.swapaxes(state_delta_conv, 1, 2)
    pool_in = jnp.swapaxes(state_pool, 1, 2)

    p_conv, p_pool = [], []
    p_s = s_s = s_conv = s_pool = None
    di = pi = 0
    for l in range(depth):
        h = rmsnorm_concat(*x, norm_mix[l], BF16) if l == 0 else rmsnorm(x, norm_mix[l], BF16)
        if l % 2 == 0:
            proj = matmul(h, w_in_delta_t, di, ncols=main_w, tn=1024, out_dtype=F32, w_transposed=True)
            tail0 = main_w // LANE
            small = matmul(h, w_in_delta_t, di, ncols=MEM_W + LANE, tn=LANE, out_dtype=F32, w_transposed=True,
                           col_blocks=tuple(range(tail0, tail0 + (MEM_W + LANE) // LANE)),
                           out_slices=((2 * n_heads, 2 * n_heads + MEM_W), (0, LANE)))
            gates = (conv_w[di], a_log[di], dt_bias[di], delta_onorm[di])
            y, conv_p, p_s = delta_mixer(proj, small, 0, zero_conv, 0, None, zero_s, di, p_s, None, y_shape, *gates,
                                         batch=bp, seq=lp, chunk=64, bb=1, unit_group=n_heads, seq_minor=False)
            y, s_conv, s_s = delta_mixer(proj, small, tp, conv_in, di, s_conv, state_delta_S, di, s_s, y, y_shape,
                                         *gates, batch=bs, seq=ls, chunk=64, bb=8, unit_group=4 * n_heads,
                                         seq_minor=True)
            p_conv.append(conv_p)
            q_src, q_blk = small, 0
            di += 1
        else:
            proj = matmul(h, w_in_pool, pi, ncols=tok_w + MEM_W, tn=1024, out_dtype=F32)
            y, pool_p = pool_mixer(proj, 0, zero_pool, 0, None, w_pool_grp, pool_scale, pi, None, y_shape,
                                   batch=bp, seq=lp, chunk=128, bb=1, n_past=0, seq_minor=False)
            y, s_pool = pool_mixer(proj, tp, pool_in, pi, s_pool, w_pool_grp, pool_scale, pi, y, y_shape,
                                   batch=bs, seq=ls, chunk=128, bb=16, n_past=n_past, seq_minor=True)
            p_pool.append(pool_p)
            q_src, q_blk = proj, tok_w // MEM_W
            pi += 1

        y = cross_attn_prompt(q_src, q_blk, p_kv[l].reshape(bp, n_mem, 2 * MEM_W), y, batch=bp, seq=lp, tq=512)
        y = cross_attn_sample(q_src, q_blk, tp, cache_k, cache_v, l, y, batch=bs, seq=ls, bb=8)

        row_tiles = (256, 128) if l == 0 else (544, 512, 256, 128)
        fits = (lambda c: tp % c == 0 and ts % c == 0) if l == 0 else (lambda c: (tp + ts) % c == 0)
        x, h2 = matmul_residual_norm(y, w_out, l, x, norm_ffn[l], tm=next(c for c in row_tiles if fits(c)))
        act = matmul(h2, w_gate_up, l, ncols=d_ff, tn=512, out_dtype=BF16, col_blocks=(0, d_ff // 512), swiglu=True,
                     tm_max=2176)
        x = matmul(act, w_down, l, ncols=d, tn=512, out_dtype=F32, residual=x, tm_max=512)

    y_p, y_s = rmsnorm_split(x, norm_final, tp)
    return (y_p.reshape(bp, lp, d), y_s.reshape(bs, ls, d),
            p_s, jnp.stack(p_conv), jnp.stack(p_pool), p_mem_k, p_mem_v,
            s_s, jnp.swapaxes(s_conv, 1, 2), jnp.swapaxes(s_pool, 1, 2))
```

```python
import functools
import math

import jax
import jax.numpy as jnp
from jax import lax
from jax.experimental import pallas as pl
from jax.experimental.pallas import tpu as pltpu

F32 = jnp.float32
BF16 = jnp.bfloat16
HIGHEST = lax.Precision.HIGHEST

EPS = 1e-6
HEAD_DIM = 128
N_MEM_HEADS = 4
MEM_W = N_MEM_HEADS * HEAD_DIM
CONV_W = 4
POOL_WINDOWS = (2, 4, 8, 16)
POOL_HIST = max(POOL_WINDOWS)
PAST_LEN = 16384
LANE = 128
SUBLANE = 8
MXU_COLS = 256
CARRY_OFF = SUBLANE - (CONV_W - 1)
VMEM_LIMIT = 56 * 1024 * 1024


def _params(*sem):
    return pltpu.CompilerParams(dimension_semantics=sem, vmem_limit_bytes=VMEM_LIMIT)


def _sigmoid(x):
    return 1.0 / (1.0 + jnp.exp(-x))


def _silu(x):
    half = 0.5 * x
    return half + half * jnp.tanh(half)


def _dot(a, b, precision=None):
    return jnp.dot(a, b, preferred_element_type=F32, precision=precision)


def _dot_nt(a, b, precision=None):
    return lax.dot_general(a, b, (((1,), (1,)), ((), ())), preferred_element_type=F32, precision=precision)


def _dot_tn(a, b, precision=None):
    return lax.dot_general(a, b, (((0,), (0,)), ((), ())), preferred_element_type=F32, precision=precision)


def _bdot(a, b):
    return _dot(a.astype(BF16), b.astype(BF16))


def _bdot_nt(a, b):
    return _dot_nt(a.astype(BF16), b.astype(BF16))


def _bdot_tn(a, b):
    return _dot_tn(a.astype(BF16), b.astype(BF16))


def _alias_args(in_specs, args, prev, out_index):
    if prev is None:
        return {}
    in_specs.append(pl.BlockSpec(memory_space=pl.ANY))
    args.append(prev)
    return {len(args) - 1: out_index}


def _rmsnorm_body(x_ref, w_ref, o_ref):
    x = x_ref[...]
    ms = jnp.mean(x * x, axis=-1, keepdims=True)
    o_ref[...] = (x * lax.rsqrt(ms + EPS) * w_ref[...]).astype(o_ref.dtype)


def rmsnorm(x, w, out_dtype):
    t, d = x.shape
    tm = next(c for c in (512, 256, 128) if t % c == 0)
    return pl.pallas_call(
        _rmsnorm_body,
        grid=(t // tm,),
        in_specs=[pl.BlockSpec((tm, d), lambda i: (i, 0)), pl.BlockSpec((1, d), lambda i: (0, 0))],
        out_specs=pl.BlockSpec((tm, d), lambda i: (i, 0)),
        out_shape=jax.ShapeDtypeStruct((t, d), out_dtype),
        compiler_params=_params("arbitrary"),
        name="rmsnorm",
    )(x, w.reshape(1, d))


def _rmsnorm_concat_body(a_ref, b_ref, w_ref, o_ref, *, n_a):
    x = jnp.where(pl.program_id(0) < n_a, a_ref[...], b_ref[...])
    ms = jnp.mean(x * x, axis=-1, keepdims=True)
    o_ref[...] = (x * lax.rsqrt(ms + EPS) * w_ref[...]).astype(o_ref.dtype)


def rmsnorm_concat(xa, xb, w, out_dtype):
    (ra, d), rb = xa.shape, xb.shape[0]
    tm = next(c for c in (512, 256, 128) if ra % c == 0 and rb % c == 0)
    n_a = ra // tm
    return pl.pallas_call(
        functools.partial(_rmsnorm_concat_body, n_a=n_a),
        grid=((ra + rb) // tm,),
        in_specs=[pl.BlockSpec((tm, d), lambda i: (jnp.minimum(i, n_a - 1), 0)),
                  pl.BlockSpec((tm, d), lambda i: (jnp.maximum(i - n_a, 0), 0)),
                  pl.BlockSpec((1, d), lambda i: (0, 0))],
        out_specs=pl.BlockSpec((tm, d), lambda i: (i, 0)),
        out_shape=jax.ShapeDtypeStruct((ra + rb, d), out_dtype),
        compiler_params=_params("arbitrary"),
        name="rmsnorm_concat",
    )(xa, xb, w.reshape(1, d))


def _rmsnorm_split_body(x_ref, w_ref, a_ref, b_ref, *, n_a):
    x = x_ref[...]
    ms = jnp.mean(x * x, axis=-1, keepdims=True)
    y = x * lax.rsqrt(ms + EPS) * w_ref[...]

    @pl.when(pl.program_id(0) < n_a)
    def _first():
        a_ref[...] = y

    @pl.when(pl.program_id(0) >= n_a)
    def _second():
        b_ref[...] = y


def rmsnorm_split(x, w, rows_a):
    t, d = x.shape
    tm = next(c for c in (512, 256, 128) if rows_a % c == 0 and (t - rows_a) % c == 0)
    n_a = rows_a // tm
    return pl.pallas_call(
        functools.partial(_rmsnorm_split_body, n_a=n_a),
        grid=(t // tm,),
        in_specs=[pl.BlockSpec((tm, d), lambda i: (i, 0)), pl.BlockSpec((1, d), lambda i: (0, 0))],
        out_specs=[pl.BlockSpec((tm, d), lambda i: (jnp.minimum(i, n_a - 1), 0)),
                   pl.BlockSpec((tm, d), lambda i: (jnp.maximum(i - n_a, 0), 0))],
        out_shape=[jax.ShapeDtypeStruct((rows_a, d), F32), jax.ShapeDtypeStruct((t - rows_a, d), F32)],
        compiler_params=_params("arbitrary"),
        name="rmsnorm_split",
    )(x, w.reshape(1, d))


def _mm_body(*refs, n_w, tn, has_res, swiglu, out_slices, slab_valid, w_transposed):
    x_ref = refs[0]
    w_refs = refs[1:1 + n_w]
    res_ref = refs[1 + n_w] if has_res else None
    o_ref = refs[-2]
    wb_ref = refs[-1]

    @pl.when(pl.program_id(1) == 0)
    def _cast_weights():
        for i, w_ref in enumerate(w_refs):
            valid = slab_valid[i]
            if w_transposed:
                wb_ref[i * tn:i * tn + valid, :] = w_ref[0:valid, :].astype(BF16)
                if valid < tn:
                    wb_ref[i * tn + valid:(i + 1) * tn, :] = jnp.zeros((tn - valid, wb_ref.shape[1]), BF16)
            else:
                wb_ref[:, i * tn:i * tn + valid] = w_ref[:, 0:valid].astype(BF16)
                if valid < tn:
                    wb_ref[:, i * tn + valid:(i + 1) * tn] = jnp.zeros((wb_ref.shape[0], tn - valid), BF16)

    def product(lo, hi):
        if w_transposed:
            return _dot_nt(x_ref[...], wb_ref[lo:hi, :])
        return _dot(x_ref[...], wb_ref[:, lo:hi])

    if out_slices is not None:
        acc = product(0, n_w * tn)
        acc = jnp.concatenate([acc[:, lo:hi] for lo, hi in out_slices], axis=1)
        o_ref[...] = acc.astype(o_ref.dtype)
        return
    for c0 in range(0, tn, MXU_COLS):
        c1 = min(c0 + MXU_COLS, tn)
        acc = product(c0, c1)
        if swiglu:
            acc = _silu(acc) * product(tn + c0, tn + c1)
        if has_res:
            acc = acc + res_ref[:, c0:c1]
        o_ref[:, c0:c1] = acc.astype(o_ref.dtype)


def matmul(x, w, layer, *, ncols, tn, out_dtype, col_blocks=(0,), residual=None, swiglu=False, out_slices=None,
           w_transposed=False, tm_max=1088):
    t, k = x.shape
    tm = next(c for c in (2176, 1088, 1024, 512, 256, 128) if c <= tm_max and t % c == 0)
    n_w = len(col_blocks)
    to = tn if out_slices is None else sum(hi - lo for lo, hi in out_slices)
    in_specs = [pl.BlockSpec((tm, k), lambda j, m: (m, 0))]
    for cb in col_blocks:
        if w_transposed:
            in_specs.append(pl.BlockSpec((None, tn, k), lambda j, m, cb=cb: (layer, cb + j, 0)))
        else:
            in_specs.append(pl.BlockSpec((None, k, tn), lambda j, m, cb=cb: (layer, 0, cb + j)))
    args = [x] + [w] * n_w
    if residual is not None:
        in_specs.append(pl.BlockSpec((tm, to), lambda j, m: (m, j)))
        args.append(residual)
    n_tiles = ncols // to
    n_total = w.shape[1] if w_transposed else w.shape[2]
    slab_valid = tuple(min(tn, n_total - (cb + n_tiles - 1) * tn) for cb in col_blocks)
    assert all(v == tn for v in slab_valid) or n_tiles == 1
    body = functools.partial(_mm_body, n_w=n_w, tn=tn, has_res=residual is not None, swiglu=swiglu,
                             out_slices=out_slices, slab_valid=slab_valid, w_transposed=w_transposed)
    return pl.pallas_call(
        body,
        grid=(n_tiles, t // tm),
        in_specs=in_specs,
        out_specs=pl.BlockSpec((tm, to), lambda j, m: (m, j)),
        out_shape=jax.ShapeDtypeStruct((t, ncols), out_dtype),
        scratch_shapes=[pltpu.VMEM((n_w * tn, k) if w_transposed else (k, n_w * tn), BF16)],
        compiler_params=_params("arbitrary", "arbitrary"),
        name="matmul",
    )(*args)


def _proj_norm_body(*refs, n_a):
    x_ref, w_ref, res_ref = refs[:3]
    res_b_ref = refs[3] if n_a is not None else None
    nw_ref, o_ref, h_ref, wb_ref = refs[-4:]

    def residual(c0):
        res = res_ref[:, c0:c0 + MXU_COLS]
        if res_b_ref is None:
            return res
        return jnp.where(pl.program_id(0) < n_a, res, res_b_ref[:, c0:c0 + MXU_COLS])

    @pl.when(pl.program_id(0) == 0)
    def _cast_weights():
        wb_ref[...] = w_ref[...].astype(BF16)

    n = o_ref.shape[1]
    sumsq = jnp.zeros((o_ref.shape[0], LANE), F32)
    for c0 in range(0, n, MXU_COLS):
        acc = _dot(x_ref[...], wb_ref[:, c0:c0 + MXU_COLS]) + residual(c0)
        o_ref[:, c0:c0 + MXU_COLS] = acc
        sq = acc * acc
        for l0 in range(0, MXU_COLS, LANE):
            sumsq = sumsq + sq[:, l0:l0 + LANE]
    scale = lax.rsqrt(jnp.sum(sumsq, axis=-1, keepdims=True) / n + EPS)
    for c0 in range(0, n, MXU_COLS):
        h = o_ref[:, c0:c0 + MXU_COLS] * scale * nw_ref[:, c0:c0 + MXU_COLS]
        h_ref[:, c0:c0 + MXU_COLS] = h.astype(h_ref.dtype)


def matmul_residual_norm(x, w, layer, residual, norm_w, *, tm):
    t, k = x.shape
    n = w.shape[2]
    in_specs = [
        pl.BlockSpec((tm, k), lambda m: (m, 0)),
        pl.BlockSpec((None, k, n), lambda m: (layer, 0, 0), pipeline_mode=pl.Buffered(1)),
    ]
    if isinstance(residual, tuple):
        n_a = residual[0].shape[0] // tm
        in_specs += [pl.BlockSpec((tm, n), lambda m: (jnp.minimum(m, n_a - 1), 0)),
                     pl.BlockSpec((tm, n), lambda m: (jnp.maximum(m - n_a, 0), 0))]
    else:
        n_a, residual = None, (residual,)
        in_specs.append(pl.BlockSpec((tm, n), lambda m: (m, 0)))
    in_specs.append(pl.BlockSpec((1, n), lambda m: (0, 0)))
    return pl.pallas_call(
        functools.partial(_proj_norm_body, n_a=n_a),
        grid=(t // tm,),
        in_specs=in_specs,
        out_specs=[pl.BlockSpec((tm, n), lambda m: (m, 0)), pl.BlockSpec((tm, n), lambda m: (m, 0))],
        out_shape=[jax.ShapeDtypeStruct((t, n), F32), jax.ShapeDtypeStruct((t, n), BF16)],
        scratch_shapes=[pltpu.VMEM((k, n), BF16)],
        compiler_params=_params("arbitrary"),
        name="matmul_residual_norm",
    )(x, w, *residual, norm_w.reshape(1, n))


def _delta_body(*refs, c_real, c_pad, bb, n_sub, n_heads, tok_w, seq_group, n_alias, seq_minor):
    main_ref, small_ref, cbuf_ref, s0_ref, cw_ref, alog_ref, dtb_ref, onw_ref = refs[:8]
    y_ref, cnew_ref, s_ref, xp_scr, ab_scr, y_scr = refs[8 + n_alias:]
    qkv_w = 3 * tok_w
    c = c_pad
    padded = c_pad > c_real
    assert not (padded and n_sub > 1)
    seq_rows = n_sub * c_real

    @pl.when(pl.program_id(1) == 0)
    def _init():
        s_ref[...] = s0_ref[...]
        for s in range(bb):
            xp_scr[s, 0:CARRY_OFF, :] = jnp.zeros((CARRY_OFF, xp_scr.shape[2]), F32)
            if seq_minor:
                for i in range(CONV_W - 1):
                    xp_scr[s, CARRY_OFF + i:CARRY_OFF + i + 1, 0:qkv_w] = cbuf_ref[i, s:s + 1, :]
            else:
                xp_scr[s, CARRY_OFF:SUBLANE, 0:qkv_w] = cbuf_ref[s]
            if padded:
                xp_scr[s, SUBLANE + c_real:SUBLANE + c, :] = jnp.zeros((c - c_real, xp_scr.shape[2]), F32)
        if padded:
            ab_scr[...] = jnp.zeros(ab_scr.shape, F32)

    if padded:
        for s in range(bb):
            xp_scr[s, SUBLANE:SUBLANE + c_real, :] = main_ref[s * c_real:(s + 1) * c_real, :]
            ab_scr[s, 0:c_real, :] = small_ref[s * c_real:(s + 1) * c_real, MEM_W:MEM_W + LANE]

    def chunk_cols(s, j, lo, hi):
        if padded:
            return xp_scr[s, SUBLANE:SUBLANE + c, lo:hi]
        r0 = s * seq_rows + j * c
        return main_ref[r0:r0 + c, lo:hi]

    def conv_rows(s, j, col):
        if padded or j == 0:
            return jnp.concatenate([xp_scr[s, 0:SUBLANE, col:col + LANE], chunk_cols(s, 0, col, col + LANE)], axis=0)
        r0 = s * seq_rows + j * c
        return main_ref[r0 - SUBLANE:r0 + c, col:col + LANE]

    def conv_slab(s, j, col):
        blk = conv_rows(s, j, col)
        n = SUBLANE + c
        acc = blk[SUBLANE:] * cw_ref[CONV_W - 1:CONV_W, col:col + LANE]
        for i in range(CONV_W - 1):
            tap = pltpu.roll(blk, n - (CARRY_OFF + i), axis=0)[:c]
            acc = acc + tap * cw_ref[i:i + 1, col:col + LANE]
        return _silu(acc)

    rows = lax.broadcasted_iota(jnp.int32, (c, c), 0)
    cols = lax.broadcasted_iota(jnp.int32, (c, c), 1)
    incl = rows >= cols
    strict = rows > cols
    eye_l = (lax.broadcasted_iota(jnp.int32, (LANE, LANE), 0)
             == lax.broadcasted_iota(jnp.int32, (LANE, LANE), 1)).astype(F32)
    n_levels = int(math.log2(c))

    for s0 in range(0, bb, seq_group):
        seqs = range(s0, min(s0 + seq_group, bb))
        chunks = [(s, j) for s in seqs for j in range(n_sub)]
        units = [(s, j, h) for (s, j) in chunks for h in range(n_heads)]

        beta_all, gc_all, gc_t, gam_all, kdec_all, gend_all = {}, {}, {}, {}, {}, {}
        for sj in chunks:
            s, j = sj
            if padded:
                ab = ab_scr[s]
            else:
                r0 = s * seq_rows + j * c
                ab = small_ref[r0:r0 + c, MEM_W:MEM_W + LANE]
            a_in = ab + dtb_ref[...]
            softplus = jnp.maximum(a_in, 0.0) + jnp.log(1.0 + jnp.exp(-jnp.abs(a_in)))
            g = -jnp.exp(alog_ref[...]) * softplus
            beta = _sigmoid(ab)
            if padded:
                live = lax.broadcasted_iota(jnp.int32, (c, LANE), 0) < c_real
                g = jnp.where(live, g, 0.0)
                beta = jnp.where(live, beta, 0.0)
            gc = _dot(incl.astype(F32), g, HIGHEST)
            g_last = gc[c - 1:c, :]
            beta_all[sj] = beta
            gc_all[sj] = gc
            gc_t[sj] = _dot_nt(eye_l, gc, HIGHEST)
            gam_all[sj] = jnp.exp(gc)
            kdec_all[sj] = jnp.exp(g_last - gc)
            gend_all[sj] = jnp.exp(g_last)

        q, k, v, beta, gam, decay = {}, {}, {}, {}, {}, {}
        for un in units:
            s, j, h = un
            lo = h * HEAD_DIM
            qh = conv_slab(s, j, lo)
            kh = conv_slab(s, j, tok_w + lo)
            v[un] = conv_slab(s, j, 2 * tok_w + lo)
            q[un] = qh * lax.rsqrt(jnp.sum(qh * qh, axis=-1, keepdims=True) + EPS) * (HEAD_DIM ** -0.5)
            k[un] = kh * lax.rsqrt(jnp.sum(kh * kh, axis=-1, keepdims=True) + EPS)
            beta[un] = beta_all[s, j][:, n_heads + h:n_heads + h + 1]
            gam[un] = gam_all[s, j][:, h:h + 1]
            diff = gc_all[s, j][:, h:h + 1] - gc_t[s, j][h:h + 1, :]
            decay[un] = jnp.where(incl, jnp.exp(jnp.where(incl, diff, 0.0)), 0.0)

        k_b = {un: k[un].astype(BF16) for un in units}
        qk_kk = {un: _dot_nt(jnp.concatenate([q[un].astype(BF16), k_b[un]], axis=0), k_b[un]) for un in units}

        x_pow = {un: -(jnp.where(strict, qk_kk[un][c:] * decay[un], 0.0) * beta[un]) for un in units}
        t_off = dict(x_pow)
        for lvl in range(1, n_levels + 1):
            for un in units:
                x_b = x_pow[un].astype(BF16)
                if lvl == 1:
                    x_pow[un] = _dot(x_b, x_b)
                elif lvl < n_levels:
                    both = _dot(jnp.concatenate([t_off[un].astype(BF16), x_b], axis=0), x_b)
                    t_off[un] = t_off[un] + x_pow[un] + both[:c]
                    x_pow[un] = both[c:]
                else:
                    t_off[un] = t_off[un] + x_pow[un] + _dot(t_off[un].astype(BF16), x_b)

        sol = {}
        for un in units:
            rhs = jnp.concatenate([v[un] * beta[un], k[un] * (beta[un] * gam[un])], axis=1)
            sol[un] = rhs + _bdot(t_off[un], rhs)

        state = {(s, h): s_ref[s, h] for s in seqs for h in range(n_heads)}
        o = {}
        for j in range(n_sub):
            now = [(s, j, h) for s in seqs for h in range(n_heads)]
            from_state = {un: _bdot(jnp.concatenate([sol[un][:, HEAD_DIM:], q[un] * gam[un]], axis=0),
                                    state[un[0], un[2]]) for un in now}
            u = {un: sol[un][:, :HEAD_DIM] - from_state[un][:c] for un in now}
            for un in now:
                o[un] = from_state[un][c:] + _bdot(jnp.where(incl, qk_kk[un][:c] * decay[un], 0.0), u[un])
            for un in now:
                s, _, h = un
                state[s, h] = (state[s, h] * gend_all[s, j][:, h:h + 1]
                               + _bdot_tn(k[un] * kdec_all[s, j][:, h:h + 1], u[un]))
        for (s, h), val in state.items():
            s_ref[s, h] = val

        for un in units:
            s, j, h = un
            lo = h * HEAD_DIM
            oh = o[un] * lax.rsqrt(jnp.mean(o[un] * o[un], axis=-1, keepdims=True) + EPS) * onw_ref[...]
            out = oh * _silu(chunk_cols(s, j, qkv_w + lo, qkv_w + lo + HEAD_DIM))
            if padded:
                y_scr[s * c_real:(s + 1) * c_real, lo:lo + HEAD_DIM] = out[:c_real]
            else:
                r0 = s * seq_rows + j * c
                y_ref[r0:r0 + c, lo:lo + HEAD_DIM] = out.astype(y_ref.dtype)

    if padded:
        y_ref[...] = y_scr[...].astype(y_ref.dtype)
    for s in range(bb):
        if padded:
            carry = xp_scr[s, CARRY_OFF + c_real:SUBLANE + c_real, 0:qkv_w]
        else:
            carry = main_ref[(s + 1) * seq_rows - (CONV_W - 1):(s + 1) * seq_rows, 0:qkv_w]
        xp_scr[s, CARRY_OFF:SUBLANE, 0:qkv_w] = carry
        if seq_minor:
            for i in range(CONV_W - 1):
                cnew_ref[i, s:s + 1, :] = carry[i:i + 1, :]
        else:
            cnew_ref[s] = carry


def delta_mixer(main, small, row_off, conv_buf, conv_layer, conv_out_prev, s0, s_layer, s_out_prev, y_prev, y_shape,
                conv_w, a_log, dt_bias, onorm_w, *, batch, seq, chunk, bb, n_sub, seq_group, seq_minor):
    n_layers, _, n_heads = s0.shape[:3]
    tok_w = n_heads * HEAD_DIM
    main_w = 4 * tok_w
    qkv_w = 3 * tok_w
    c_real = min(chunk, seq)
    c_pad = max(c_real, SUBLANE)
    assert bb == 1 or c_real == seq
    nch = seq // (n_sub * c_real)
    rows = bb * n_sub * c_real
    blk0 = row_off // rows
    row_spec = lambda w: pl.BlockSpec((rows, w), lambda b, ch: (blk0 + b * nch + ch, 0))
    pad_lane = lambda p: jnp.zeros((1, LANE), F32).at[0, :n_heads].set(p.astype(F32))
    s_spec = pl.BlockSpec((None, bb, n_heads, HEAD_DIM, HEAD_DIM), lambda b, ch: (s_layer, b, 0, 0, 0))
    if seq_minor:
        conv_in_spec = pl.BlockSpec((None, CONV_W - 1, bb, qkv_w), lambda b, ch: (conv_layer, 0, b, 0))
        conv_out_spec = conv_in_spec
        conv_out_shape = (conv_buf.shape[0], CONV_W - 1, batch, qkv_w)
    else:
        conv_in_spec = pl.BlockSpec((None, bb, CONV_W - 1, qkv_w), lambda b, ch: (conv_layer, b, 0, 0))
        conv_out_spec = pl.BlockSpec((bb, CONV_W - 1, qkv_w), lambda b, ch: (b, 0, 0))
        conv_out_shape = (batch, CONV_W - 1, qkv_w)
    in_specs = [
        row_spec(main_w),
        row_spec(small.shape[-1]),
        conv_in_spec,
        s_spec,
        pl.BlockSpec((CONV_W, qkv_w), lambda b, ch: (0, 0)),
        pl.BlockSpec((1, LANE), lambda b, ch: (0, 0)),
        pl.BlockSpec((1, LANE), lambda b, ch: (0, 0)),
        pl.BlockSpec((1, HEAD_DIM), lambda b, ch: (0, 0)),
    ]
    args = [main, small, conv_buf, s0, conv_w, pad_lane(a_log), pad_lane(dt_bias), onorm_w.reshape(1, HEAD_DIM)]
    aliases = {}
    aliases.update(_alias_args(in_specs, args, y_prev, 0))
    aliases.update(_alias_args(in_specs, args, conv_out_prev, 1))
    aliases.update(_alias_args(in_specs, args, s_out_prev, 2))
    body = functools.partial(_delta_body, c_real=c_real, c_pad=c_pad, bb=bb, n_sub=n_sub, n_heads=n_heads,
                             tok_w=tok_w, seq_group=seq_group, n_alias=len(aliases), seq_minor=seq_minor)
    return pl.pallas_call(
        body,
        grid=(batch // bb, nch),
        in_specs=in_specs,
        out_specs=[
            row_spec(tok_w),
            conv_out_spec,
            s_spec,
        ],
        out_shape=[
            jax.ShapeDtypeStruct(y_shape, BF16),
            jax.ShapeDtypeStruct(conv_out_shape, F32),
            jax.ShapeDtypeStruct((n_layers, batch, n_heads, HEAD_DIM, HEAD_DIM), F32),
        ],
        scratch_shapes=[pltpu.VMEM((bb, SUBLANE + c_pad, main_w), F32), pltpu.VMEM((bb, c_pad, LANE), F32),
                        pltpu.VMEM((bb * c_real, tok_w), F32)],
        input_output_aliases=aliases,
        compiler_params=_params("arbitrary", "arbitrary"),
        name="delta_mixer",
    )(*args)


def _pool_body(*refs, c_real, c_pad, bb, n_past, tok_w, n_alias, seq_minor):
    u_ref, buf_ref, wg_ref, scale_ref = refs[:4]
    y_ref, pnew_ref, hist_scr, y_scr = refs[4 + n_alias:]
    c = c_pad
    padded = c_pad > c_real
    gw = tok_w // len(POOL_WINDOWS)

    @pl.when(pl.program_id(1) == 0)
    def _init():
        for s in range(bb):
            hist_scr[s, 0:1, :] = jnp.zeros((1, tok_w), F32)
            if seq_minor:
                for j in range(POOL_HIST - 1):
                    hist_scr[s, 1 + j:2 + j, :] = buf_ref[j, s:s + 1, :]
            else:
                hist_scr[s, 1:POOL_HIST, :] = buf_ref[s]
            if padded:
                hist_scr[s, POOL_HIST + c_real:POOL_HIST + c, :] = jnp.zeros((c - c_real, tok_w), F32)

    for s in range(bb):
        hist_scr[s, POOL_HIST:POOL_HIST + c_real, :] = u_ref[s * c_real:(s + 1) * c_real, :]
    pos = pl.program_id(1) * c_real + lax.broadcasted_iota(jnp.int32, (c, 1), 0)
    for gi, win in enumerate(POOL_WINDOWS):
        lo = gi * gw
        inv_cnt = 1.0 / jnp.minimum(win, pos + 1 + n_past).astype(F32)
        ds = []
        for s in range(bb):
            tok = hist_scr[s, POOL_HIST:POOL_HIST + c, lo:lo + gw]
            acc = tok
            for back in range(1, win):
                acc = acc + hist_scr[s, POOL_HIST - back:POOL_HIST - back + c, lo:lo + gw]
            ds.append(acc * inv_cnt - tok)
        d_all = ds[0] if bb == 1 else jnp.concatenate(ds, axis=0)
        y_all = _bdot(d_all, wg_ref[gi]) * scale_ref[:, lo:lo + gw]
        if padded:
            for s in range(bb):
                y_scr[s * c_real:(s + 1) * c_real, lo:lo + gw] = y_all[s * c:s * c + c_real]
        else:
            y_ref[:, lo:lo + gw] = y_all.astype(y_ref.dtype)

    if padded:
        y_ref[...] = y_scr[...].astype(y_ref.dtype)
    for s in range(bb):
        tail = hist_scr[s, c_real + 1:c_real + POOL_HIST, :]
        hist_scr[s, 1:POOL_HIST, :] = tail
        if seq_minor:
            for j in range(POOL_HIST - 1):
                pnew_ref[j, s:s + 1, :] = tail[j:j + 1, :]
        else:
            pnew_ref[s] = tail


def pool_mixer(u, row_off, buf, buf_layer, buf_out_prev, w_grp, scale, layer, y_prev, y_shape, *,
               batch, seq, chunk, bb, n_past, seq_minor):
    tok_w = buf.shape[-1]
    gw = tok_w // len(POOL_WINDOWS)
    c_real = min(chunk, seq)
    c_pad = max(c_real, SUBLANE)
    assert bb == 1 or c_real == seq
    nch = seq // c_real
    rows = bb * c_real
    blk0 = row_off // rows
    row_spec = pl.BlockSpec((rows, tok_w), lambda b, ch: (blk0 + b * nch + ch, 0))
    if seq_minor:
        buf_in_spec = pl.BlockSpec((None, POOL_HIST - 1, bb, tok_w), lambda b, ch: (buf_layer, 0, b, 0))
        buf_out_spec = buf_in_spec
        buf_out_shape = (buf.shape[0], POOL_HIST - 1, batch, tok_w)
    else:
        buf_in_spec = pl.BlockSpec((None, bb, POOL_HIST - 1, tok_w), lambda b, ch: (buf_layer, b, 0, 0))
        buf_out_spec = pl.BlockSpec((bb, POOL_HIST - 1, tok_w), lambda b, ch: (b, 0, 0))
        buf_out_shape = (batch, POOL_HIST - 1, tok_w)
    in_specs = [
        row_spec,
        buf_in_spec,
        pl.BlockSpec((None, len(POOL_WINDOWS), gw, gw), lambda b, ch: (layer, 0, 0, 0)),
        pl.BlockSpec((None, 1, tok_w), lambda b, ch: (layer, 0, 0)),
    ]
    args = [u, buf, w_grp, scale.reshape(scale.shape[0], 1, tok_w)]
    aliases = _alias_args(in_specs, args, y_prev, 0)
    aliases.update(_alias_args(in_specs, args, buf_out_prev, 1))
    body = functools.partial(_pool_body, c_real=c_real, c_pad=c_pad, bb=bb, n_past=n_past, tok_w=tok_w,
                             n_alias=len(aliases), seq_minor=seq_minor)
    return pl.pallas_call(
        body,
        grid=(batch // bb, nch),
        in_specs=in_specs,
        out_specs=[row_spec, buf_out_spec],
        out_shape=[jax.ShapeDtypeStruct(y_shape, BF16), jax.ShapeDtypeStruct(buf_out_shape, F32)],
        scratch_shapes=[pltpu.VMEM((bb, POOL_HIST + c_pad, tok_w), F32), pltpu.VMEM((rows, tok_w), F32)],
        input_output_aliases=aliases,
        compiler_params=_params("arbitrary", "arbitrary"),
        name="pool_mixer",
    )(*args)


def _softmax_rows(s):
    p = jnp.exp(s - jnp.max(s, axis=-1, keepdims=True))
    return p / jnp.sum(p, axis=-1, keepdims=True)


def _xattn_prompt_body(q_ref, k_ref, v_ref, y_prev_ref, o_ref):
    del y_prev_ref
    for h in range(N_MEM_HEADS):
        lo = h * HEAD_DIM
        s = _bdot_nt(q_ref[:, lo:lo + HEAD_DIM], k_ref[:, lo:lo + HEAD_DIM]) * (HEAD_DIM ** -0.5)
        o = _bdot(_softmax_rows(s), v_ref[:, lo:lo + HEAD_DIM])
        o_ref[:, lo:lo + HEAD_DIM] = o.astype(o_ref.dtype)


def cross_attn_prompt(q, q_col_block, kv, y_prev, *, batch, seq, tq):
    n_mem = kv.shape[1]
    tq = min(tq, seq)
    nq = seq // tq
    y_col_block = y_prev.shape[1] // MEM_W - 1
    return pl.pallas_call(
        _xattn_prompt_body,
        grid=(batch, nq),
        in_specs=[
            pl.BlockSpec((tq, MEM_W), lambda b, i: (b * nq + i, q_col_block)),
            pl.BlockSpec((None, n_mem, MEM_W), lambda b, i: (b, 0, 0)),
            pl.BlockSpec((None, n_mem, MEM_W), lambda b, i: (b, 0, 1)),
            pl.BlockSpec(memory_space=pl.ANY),
        ],
        out_specs=pl.BlockSpec((tq, MEM_W), lambda b, i: (b * nq + i, y_col_block)),
        out_shape=jax.ShapeDtypeStruct(y_prev.shape, y_prev.dtype),
        input_output_aliases={3: 0},
        compiler_params=_params("arbitrary", "arbitrary"),
        name="cross_attn_prompt",
    )(q, kv, kv, y_prev)


def _xattn_sample_body(q_ref, k_ref, v_ref, y_prev_ref, o_ref, q_scr, o_scr, *, bb, seq):
    del y_prev_ref
    n_rows = k_ref.shape[1]
    q_scr[...] = jnp.zeros(q_scr.shape, F32)
    for b in range(bb):
        q_scr[b, 0:seq, :] = q_ref[b * seq:(b + 1) * seq, :]
    shape = (N_MEM_HEADS * SUBLANE, n_rows)
    same_head = (lax.broadcasted_iota(jnp.int32, shape, 1) % N_MEM_HEADS
                 == lax.broadcasted_iota(jnp.int32, shape, 0) // SUBLANE)
    scores = []
    for b in range(bb):
        qb = q_scr[b]
        q_rows = jnp.concatenate([qb[:, h * HEAD_DIM:(h + 1) * HEAD_DIM] for h in range(N_MEM_HEADS)], axis=0)
        scores.append(_bdot_nt(q_rows, k_ref[b]) * (HEAD_DIM ** -0.5))
    probs = []
    for s in scores:
        p = jnp.exp(s - jnp.max(jnp.where(same_head, s, -jnp.inf), axis=-1, keepdims=True))
        p = jnp.where(same_head, p, 0.0)
        probs.append(p / jnp.sum(p, axis=-1, keepdims=True))
    outs = [_bdot(p, v_ref[b]) for b, p in enumerate(probs)]
    for b, o in enumerate(outs):
        for h in range(N_MEM_HEADS):
            o_scr[b * seq:(b + 1) * seq, h * HEAD_DIM:(h + 1) * HEAD_DIM] = o[h * SUBLANE:h * SUBLANE + seq]
    o_ref[...] = o_scr[...].astype(o_ref.dtype)


def cross_attn_sample(q, q_col_block, row_off, cache_k, cache_v, layer, y_prev, *, batch, seq, bb):
    n_rows = cache_k.shape[2]
    rows = bb * seq
    blk0 = row_off // rows
    y_col_block = y_prev.shape[1] // MEM_W - 1
    kv_spec = pl.BlockSpec((None, bb, n_rows, HEAD_DIM), lambda i: (layer, i, 0, 0))
    return pl.pallas_call(
        functools.partial(_xattn_sample_body, bb=bb, seq=seq),
        grid=(batch // bb,),
        in_specs=[pl.BlockSpec((rows, MEM_W), lambda i: (blk0 + i, q_col_block)), kv_spec, kv_spec,
                  pl.BlockSpec(memory_space=pl.ANY)],
        out_specs=pl.BlockSpec((rows, MEM_W), lambda i: (blk0 + i, y_col_block)),
        out_shape=jax.ShapeDtypeStruct(y_prev.shape, y_prev.dtype),
        scratch_shapes=[pltpu.VMEM((bb, SUBLANE, MEM_W), F32), pltpu.VMEM((rows, MEM_W), F32)],
        input_output_aliases={3: 0},
        compiler_params=_params("arbitrary"),
        name="cross_attn_sample",
    )(q, cache_k, cache_v, y_prev)


def kernel(x_prompt, x_sample, mem_prompt, state_delta_S, state_delta_conv, state_pool, cache_mem_k, cache_mem_v,
           norm_mix, norm_ffn, norm_mem, norm_final, w_in_delta, conv_w, a_log, dt_bias, delta_onorm,
           w_in_pool, w_pool_grp, pool_scale, w_mem_kv, w_out, w_gate_up, w_down):
    bp, lp, d = x_prompt.shape
    bs, ls, _ = x_sample.shape
    depth = w_out.shape[0]
    n_mem = mem_prompt.shape[1]
    n_delta, _, n_heads = state_delta_S.shape[:3]
    tok_w = n_heads * HEAD_DIM
    main_w = 4 * tok_w
    d_ff = w_down.shape[1]
    tp, ts = bp * lp, bs * ls
    n_past = min(PAST_LEN, POOL_HIST - 1)
    y_shape = (tp + ts, tok_w + MEM_W)

    x = (x_prompt.reshape(tp, d), x_sample.reshape(ts, d))

    mem_rows = mem_prompt.reshape(bp * n_mem, d)
    p_kv = [matmul(rmsnorm(mem_rows, norm_mem[l], BF16), w_mem_kv, l, ncols=2 * MEM_W, tn=MEM_W, out_dtype=F32)
            for l in range(depth)]
    p_mem_k = jnp.stack([kv[:, :MEM_W].reshape(bp, n_mem, N_MEM_HEADS, HEAD_DIM) for kv in p_kv])
    p_mem_v = jnp.stack([kv[:, MEM_W:].reshape(bp, n_mem, N_MEM_HEADS, HEAD_DIM) for kv in p_kv])
    cache_k = cache_mem_k.reshape(depth, bs, n_mem * N_MEM_HEADS, HEAD_DIM)
    cache_v = cache_mem_v.reshape(depth, bs, n_mem * N_MEM_HEADS, HEAD_DIM)

    w_in_delta_t = jnp.swapaxes(w_in_delta, 1, 2)
    zero_s = jnp.zeros((n_delta, bp, n_heads, HEAD_DIM, HEAD_DIM), F32)
    zero_conv = jnp.zeros((1, bp, CONV_W - 1, 3 * tok_w), F32)
    zero_pool = jnp.zeros((1, bp, POOL_HIST - 1, tok_w), F32)
    conv_in = jnp.swapaxes(state_delta_conv, 1, 2)
    pool_in = jnp.swapaxes(state_pool, 1, 2)

    p_conv, p_pool = [], []
    p_s = s_s = s_conv = s_pool = None
    di = pi = 0
    for l in range(depth):
        h = rmsnorm_concat(*x, norm_mix[l], BF16) if l == 0 else rmsnorm(x, norm_mix[l], BF16)
        if l % 2 == 0:
            proj = matmul(h, w_in_delta_t, di, ncols=main_w, tn=1024, out_dtype=F32, w_transposed=True)
            tail0 = main_w // LANE
            small = matmul(h, w_in_delta_t, di, ncols=MEM_W + LANE, tn=LANE, out_dtype=F32, w_transposed=True,
                           col_blocks=tuple(range(tail0, tail0 + (MEM_W + LANE) // LANE)),
                           out_slices=((2 * n_heads, 2 * n_heads + MEM_W), (0, LANE)))
            gates = (conv_w[di], a_log[di], dt_bias[di], delta_onorm[di])
            y, conv_p, p_s = delta_mixer(proj, small, 0, zero_conv, 0, None, zero_s, di, p_s, None, y_shape, *gates,
                                         batch=bp, seq=lp, chunk=64, bb=1, n_sub=2, seq_group=1, seq_minor=False)
            y, s_conv, s_s = delta_mixer(proj, small, tp, conv_in, di, s_conv, state_delta_S, di, s_s, y, y_shape,
                                         *gates, batch=bs, seq=ls, chunk=64, bb=8, n_sub=1, seq_group=4,
                                         seq_minor=True)
            p_conv.append(conv_p)
            q_src, q_blk = small, 0
            di += 1
        else:
            proj = matmul(h, w_in_pool, pi, ncols=tok_w + MEM_W, tn=1024, out_dtype=F32)
            y, pool_p = pool_mixer(proj, 0, zero_pool, 0, None, w_pool_grp, pool_scale, pi, None, y_shape,
                                   batch=bp, seq=lp, chunk=256, bb=1, n_past=0, seq_minor=False)
            y, s_pool = pool_mixer(proj, tp, pool_in, pi, s_pool, w_pool_grp, pool_scale, pi, y, y_shape,
                                   batch=bs, seq=ls, chunk=128, bb=16, n_past=n_past, seq_minor=True)
            p_pool.append(pool_p)
            q_src, q_blk = proj, tok_w // MEM_W
            pi += 1

        y = cross_attn_prompt(q_src, q_blk, p_kv[l].reshape(bp, n_mem, 2 * MEM_W), y, batch=bp, seq=lp, tq=512)
        y = cross_attn_sample(q_src, q_blk, tp, cache_k, cache_v, l, y, batch=bs, seq=ls, bb=8)

        row_tiles = (256, 128) if l == 0 else (544, 512, 256, 128)
        fits = (lambda c: tp % c == 0 and ts % c == 0) if l == 0 else (lambda c: (tp + ts) % c == 0)
        x, h2 = matmul_residual_norm(y, w_out, l, x, norm_ffn[l], tm=next(c for c in row_tiles if fits(c)))
        act = matmul(h2, w_gate_up, l, ncols=d_ff, tn=512, out_dtype=BF16, col_blocks=(0, d_ff // 512), swiglu=True,
                     tm_max=2176)
        x = matmul(act, w_down, l, ncols=d, tn=512, out_dtype=F32, residual=x, tm_max=512)

    y_p, y_s = rmsnorm_split(x, norm_final, tp)
    return (y_p.reshape(bp, lp, d), y_s.reshape(bs, ls, d),
            p_s, jnp.stack(p_conv), jnp.stack(p_pool), p_mem_k, p_mem_v,
            s_s, jnp.swapaxes(s_conv, 1, 2), jnp.swapaxes(s_pool, 1, 2))
```

```python
import functools
import math

import jax
import jax.numpy as jnp
from jax import lax
from jax.experimental import pallas as pl
from jax.experimental.pallas import tpu as pltpu

F32 = jnp.float32
BF16 = jnp.bfloat16
HIGHEST = lax.Precision.HIGHEST

EPS = 1e-6
HEAD_DIM = 128
N_MEM_HEADS = 4
MEM_W = N_MEM_HEADS * HEAD_DIM
CONV_W = 4
POOL_WINDOWS = (2, 4, 8, 16)
POOL_HIST = max(POOL_WINDOWS)
PAST_LEN = 16384
LANE = 128
SUBLANE = 8
MXU_COLS = 256
CARRY_OFF = SUBLANE - (CONV_W - 1)
VMEM_LIMIT = 56 * 1024 * 1024


def _params(*sem):
    return pltpu.CompilerParams(dimension_semantics=sem, vmem_limit_bytes=VMEM_LIMIT)


def _sigmoid(x):
    return 1.0 / (1.0 + jnp.exp(-x))


def _silu(x):
    half = 0.5 * x
    return half + half * jnp.tanh(half)


def _dot(a, b, precision=None):
    return jnp.dot(a, b, preferred_element_type=F32, precision=precision)


def _dot_nt(a, b, precision=None):
    return lax.dot_general(a, b, (((1,), (1,)), ((), ())), preferred_element_type=F32, precision=precision)


def _dot_tn(a, b, precision=None):
    return lax.dot_general(a, b, (((0,), (0,)), ((), ())), preferred_element_type=F32, precision=precision)


def _bdot(a, b):
    return _dot(a.astype(BF16), b.astype(BF16))


def _bdot_nt(a, b):
    return _dot_nt(a.astype(BF16), b.astype(BF16))


def _bdot_tn(a, b):
    return _dot_tn(a.astype(BF16), b.astype(BF16))


def _alias_args(in_specs, args, prev, out_index):
    if prev is None:
        return {}
    in_specs.append(pl.BlockSpec(memory_space=pl.ANY))
    args.append(prev)
    return {len(args) - 1: out_index}


def _rmsnorm_body(x_ref, w_ref, o_ref):
    x = x_ref[...]
    ms = jnp.mean(x * x, axis=-1, keepdims=True)
    o_ref[...] = (x * lax.rsqrt(ms + EPS) * w_ref[...]).astype(o_ref.dtype)


def rmsnorm(x, w, out_dtype):
    t, d = x.shape
    tm = next(c for c in (1088, 1024, 512, 256, 128) if t % c == 0)
    return pl.pallas_call(
        _rmsnorm_body,
        grid=(t // tm,),
        in_specs=[pl.BlockSpec((tm, d), lambda i: (i, 0)), pl.BlockSpec((1, d), lambda i: (0, 0))],
        out_specs=pl.BlockSpec((tm, d), lambda i: (i, 0)),
        out_shape=jax.ShapeDtypeStruct((t, d), out_dtype),
        compiler_params=_params("arbitrary"),
        name="rmsnorm",
    )(x, w.reshape(1, d))


def _rmsnorm_concat_body(a_ref, b_ref, w_ref, o_ref, *, n_a):
    x = jnp.where(pl.program_id(0) < n_a, a_ref[...], b_ref[...])
    ms = jnp.mean(x * x, axis=-1, keepdims=True)
    o_ref[...] = (x * lax.rsqrt(ms + EPS) * w_ref[...]).astype(o_ref.dtype)


def rmsnorm_concat(xa, xb, w, out_dtype):
    (ra, d), rb = xa.shape, xb.shape[0]
    tm = next(c for c in (512, 256, 128) if ra % c == 0 and rb % c == 0)
    n_a = ra // tm
    return pl.pallas_call(
        functools.partial(_rmsnorm_concat_body, n_a=n_a),
        grid=((ra + rb) // tm,),
        in_specs=[pl.BlockSpec((tm, d), lambda i: (jnp.minimum(i, n_a - 1), 0)),
                  pl.BlockSpec((tm, d), lambda i: (jnp.maximum(i - n_a, 0), 0)),
                  pl.BlockSpec((1, d), lambda i: (0, 0))],
        out_specs=pl.BlockSpec((tm, d), lambda i: (i, 0)),
        out_shape=jax.ShapeDtypeStruct((ra + rb, d), out_dtype),
        compiler_params=_params("arbitrary"),
        name="rmsnorm_concat",
    )(xa, xb, w.reshape(1, d))


def _rmsnorm_split_body(x_ref, w_ref, a_ref, b_ref, *, n_a):
    x = x_ref[...]
    ms = jnp.mean(x * x, axis=-1, keepdims=True)
    y = x * lax.rsqrt(ms + EPS) * w_ref[...]

    @pl.when(pl.program_id(0) < n_a)
    def _first():
        a_ref[...] = y

    @pl.when(pl.program_id(0) >= n_a)
    def _second():
        b_ref[...] = y


def rmsnorm_split(x, w, rows_a):
    t, d = x.shape
    tm = next(c for c in (512, 256, 128) if rows_a % c == 0 and (t - rows_a) % c == 0)
    n_a = rows_a // tm
    return pl.pallas_call(
        functools.partial(_rmsnorm_split_body, n_a=n_a),
        grid=(t // tm,),
        in_specs=[pl.BlockSpec((tm, d), lambda i: (i, 0)), pl.BlockSpec((1, d), lambda i: (0, 0))],
        out_specs=[pl.BlockSpec((tm, d), lambda i: (jnp.minimum(i, n_a - 1), 0)),
                   pl.BlockSpec((tm, d), lambda i: (jnp.maximum(i - n_a, 0), 0))],
        out_shape=[jax.ShapeDtypeStruct((rows_a, d), F32), jax.ShapeDtypeStruct((t - rows_a, d), F32)],
        compiler_params=_params("arbitrary"),
        name="rmsnorm_split",
    )(x, w.reshape(1, d))


def _mm_body(*refs, n_w, tn, has_res, swiglu, out_slices, slab_valid, w_transposed):
    x_ref = refs[0]
    w_refs = refs[1:1 + n_w]
    res_ref = refs[1 + n_w] if has_res else None
    o_ref = refs[-2]
    wb_ref = refs[-1]

    @pl.when(pl.program_id(1) == 0)
    def _cast_weights():
        for i, w_ref in enumerate(w_refs):
            valid = slab_valid[i]
            if w_transposed:
                wb_ref[i * tn:i * tn + valid, :] = w_ref[0:valid, :].astype(BF16)
                if valid < tn:
                    wb_ref[i * tn + valid:(i + 1) * tn, :] = jnp.zeros((tn - valid, wb_ref.shape[1]), BF16)
            else:
                wb_ref[:, i * tn:i * tn + valid] = w_ref[:, 0:valid].astype(BF16)
                if valid < tn:
                    wb_ref[:, i * tn + valid:(i + 1) * tn] = jnp.zeros((wb_ref.shape[0], tn - valid), BF16)

    def product(lo, hi):
        if w_transposed:
            return _dot_nt(x_ref[...], wb_ref[lo:hi, :])
        return _dot(x_ref[...], wb_ref[:, lo:hi])

    if out_slices is not None:
        acc = product(0, n_w * tn)
        acc = jnp.concatenate([acc[:, lo:hi] for lo, hi in out_slices], axis=1)
        o_ref[...] = acc.astype(o_ref.dtype)
        return
    for c0 in range(0, tn, MXU_COLS):
        c1 = min(c0 + MXU_COLS, tn)
        acc = product(c0, c1)
        if swiglu:
            acc = _silu(acc) * product(tn + c0, tn + c1)
        if has_res:
            acc = acc + res_ref[:, c0:c1]
        o_ref[:, c0:c1] = acc.astype(o_ref.dtype)


def matmul(x, w, layer, *, ncols, tn, out_dtype, col_blocks=(0,), residual=None, swiglu=False, out_slices=None,
           w_transposed=False, tm_max=1088, w_double_buffer=True):
    t, k = x.shape
    tm = next(c for c in (2176, 1088, 1024, 512, 256, 128) if c <= tm_max and t % c == 0)
    n_w = len(col_blocks)
    to = tn if out_slices is None else sum(hi - lo for lo, hi in out_slices)
    in_specs = [pl.BlockSpec((tm, k), lambda j, m: (m, 0))]
    for cb in col_blocks:
        if w_transposed:
            in_specs.append(pl.BlockSpec((None, tn, k), lambda j, m, cb=cb: (layer, cb + j, 0)))
        else:
            in_specs.append(pl.BlockSpec((None, k, tn), lambda j, m, cb=cb: (layer, 0, cb + j),
                                         **({} if w_double_buffer else {"pipeline_mode": pl.Buffered(1)})))
    args = [x] + [w] * n_w
    if residual is not None:
        in_specs.append(pl.BlockSpec((tm, to), lambda j, m: (m, j)))
        args.append(residual)
    n_tiles = ncols // to
    n_total = w.shape[1] if w_transposed else w.shape[2]
    slab_valid = tuple(min(tn, n_total - (cb + n_tiles - 1) * tn) for cb in col_blocks)
    assert all(v == tn for v in slab_valid) or n_tiles == 1
    body = functools.partial(_mm_body, n_w=n_w, tn=tn, has_res=residual is not None, swiglu=swiglu,
                             out_slices=out_slices, slab_valid=slab_valid, w_transposed=w_transposed)
    return pl.pallas_call(
        body,
        grid=(n_tiles, t // tm),
        in_specs=in_specs,
        out_specs=pl.BlockSpec((tm, to), lambda j, m: (m, j)),
        out_shape=jax.ShapeDtypeStruct((t, ncols), out_dtype),
        scratch_shapes=[pltpu.VMEM((n_w * tn, k) if w_transposed else (k, n_w * tn), BF16)],
        compiler_params=_params("arbitrary", "arbitrary"),
        name="matmul",
    )(*args)


def _proj_norm_body(*refs, n_a):
    x_ref, w_ref, res_ref = refs[:3]
    res_b_ref = refs[3] if n_a is not None else None
    nw_ref, o_ref, h_ref, wb_ref = refs[-4:]

    def residual(c0):
        res = res_ref[:, c0:c0 + MXU_COLS]
        if res_b_ref is None:
            return res
        return jnp.where(pl.program_id(0) < n_a, res, res_b_ref[:, c0:c0 + MXU_COLS])

    @pl.when(pl.program_id(0) == 0)
    def _cast_weights():
        wb_ref[...] = w_ref[...].astype(BF16)

    n = o_ref.shape[1]
    sumsq = jnp.zeros((o_ref.shape[0], LANE), F32)
    for c0 in range(0, n, MXU_COLS):
        acc = _dot(x_ref[...], wb_ref[:, c0:c0 + MXU_COLS]) + residual(c0)
        o_ref[:, c0:c0 + MXU_COLS] = acc
        sq = acc * acc
        for l0 in range(0, MXU_COLS, LANE):
            sumsq = sumsq + sq[:, l0:l0 + LANE]
    scale = lax.rsqrt(jnp.sum(sumsq, axis=-1, keepdims=True) / n + EPS)
    for c0 in range(0, n, MXU_COLS):
        h = o_ref[:, c0:c0 + MXU_COLS] * scale * nw_ref[:, c0:c0 + MXU_COLS]
        h_ref[:, c0:c0 + MXU_COLS] = h.astype(h_ref.dtype)


def matmul_residual_norm(x, w, layer, residual, norm_w, *, tm):
    t, k = x.shape
    n = w.shape[2]
    in_specs = [
        pl.BlockSpec((tm, k), lambda m: (m, 0)),
        pl.BlockSpec((None, k, n), lambda m: (layer, 0, 0), pipeline_mode=pl.Buffered(1)),
    ]
    if isinstance(residual, tuple):
        n_a = residual[0].shape[0] // tm
        in_specs += [pl.BlockSpec((tm, n), lambda m: (jnp.minimum(m, n_a - 1), 0)),
                     pl.BlockSpec((tm, n), lambda m: (jnp.maximum(m - n_a, 0), 0))]
    else:
        n_a, residual = None, (residual,)
        in_specs.append(pl.BlockSpec((tm, n), lambda m: (m, 0)))
    in_specs.append(pl.BlockSpec((1, n), lambda m: (0, 0)))
    return pl.pallas_call(
        functools.partial(_proj_norm_body, n_a=n_a),
        grid=(t // tm,),
        in_specs=in_specs,
        out_specs=[pl.BlockSpec((tm, n), lambda m: (m, 0)), pl.BlockSpec((tm, n), lambda m: (m, 0))],
        out_shape=[jax.ShapeDtypeStruct((t, n), F32), jax.ShapeDtypeStruct((t, n), BF16)],
        scratch_shapes=[pltpu.VMEM((k, n), BF16)],
        compiler_params=_params("arbitrary"),
        name="matmul_residual_norm",
    )(x, w, *residual, norm_w.reshape(1, n))


def _delta_body(*refs, c_real, c_pad, bb, n_sub, n_heads, tok_w, seq_group, n_alias, seq_minor):
    main_ref, small_ref, cbuf_ref, s0_ref, cw_ref, alog_ref, dtb_ref, onw_ref = refs[:8]
    y_ref, cnew_ref, s_ref, xp_scr, ab_scr, y_scr = refs[8 + n_alias:]
    qkv_w = 3 * tok_w
    c = c_pad
    padded = c_pad > c_real
    assert not (padded and n_sub > 1)
    seq_rows = n_sub * c_real

    @pl.when(pl.program_id(1) == 0)
    def _init():
        s_ref[...] = s0_ref[...]
        for s in range(bb):
            xp_scr[s, 0:CARRY_OFF, :] = jnp.zeros((CARRY_OFF, xp_scr.shape[2]), F32)
            if seq_minor:
                for i in range(CONV_W - 1):
                    xp_scr[s, CARRY_OFF + i:CARRY_OFF + i + 1, 0:qkv_w] = cbuf_ref[i, s:s + 1, :]
            else:
                xp_scr[s, CARRY_OFF:SUBLANE, 0:qkv_w] = cbuf_ref[s]
            if padded:
                xp_scr[s, SUBLANE + c_real:SUBLANE + c, :] = jnp.zeros((c - c_real, xp_scr.shape[2]), F32)
        if padded:
            ab_scr[...] = jnp.zeros(ab_scr.shape, F32)

    if padded:
        for s in range(bb):
            xp_scr[s, SUBLANE:SUBLANE + c_real, :] = main_ref[s * c_real:(s + 1) * c_real, :]
            ab_scr[s, 0:c_real, :] = small_ref[s * c_real:(s + 1) * c_real, MEM_W:MEM_W + LANE]

    def chunk_cols(s, j, lo, hi):
        if padded:
            return xp_scr[s, SUBLANE:SUBLANE + c, lo:hi]
        r0 = s * seq_rows + j * c
        return main_ref[r0:r0 + c, lo:hi]

    def conv_rows(s, j, col):
        if padded or j == 0:
            return jnp.concatenate([xp_scr[s, 0:SUBLANE, col:col + LANE], chunk_cols(s, 0, col, col + LANE)], axis=0)
        r0 = s * seq_rows + j * c
        return main_ref[r0 - SUBLANE:r0 + c, col:col + LANE]

    def conv_slab(s, j, col):
        blk = conv_rows(s, j, col)
        n = SUBLANE + c
        acc = blk[SUBLANE:] * cw_ref[CONV_W - 1:CONV_W, col:col + LANE]
        for i in range(CONV_W - 1):
            tap = pltpu.roll(blk, n - (CARRY_OFF + i), axis=0)[:c]
            acc = acc + tap * cw_ref[i:i + 1, col:col + LANE]
        return _silu(acc)

    rows = lax.broadcasted_iota(jnp.int32, (c, c), 0)
    cols = lax.broadcasted_iota(jnp.int32, (c, c), 1)
    incl = rows >= cols
    strict = rows > cols
    eye_l = (lax.broadcasted_iota(jnp.int32, (LANE, LANE), 0)
             == lax.broadcasted_iota(jnp.int32, (LANE, LANE), 1)).astype(F32)
    n_levels = int(math.log2(c))

    for s0 in range(0, bb, seq_group):
        seqs = range(s0, min(s0 + seq_group, bb))
        chunks = [(s, j) for s in seqs for j in range(n_sub)]
        units = [(s, j, h) for (s, j) in chunks for h in range(n_heads)]

        beta_all, gc_all, gc_t, gam_all, kdec_all, gend_all = {}, {}, {}, {}, {}, {}
        for sj in chunks:
            s, j = sj
            if padded:
                ab = ab_scr[s]
            else:
                r0 = s * seq_rows + j * c
                ab = small_ref[r0:r0 + c, MEM_W:MEM_W + LANE]
            a_in = ab + dtb_ref[...]
            softplus = jnp.maximum(a_in, 0.0) + jnp.log(1.0 + jnp.exp(-jnp.abs(a_in)))
            g = -jnp.exp(alog_ref[...]) * softplus
            beta = _sigmoid(ab)
            if padded:
                live = lax.broadcasted_iota(jnp.int32, (c, LANE), 0) < c_real
                g = jnp.where(live, g, 0.0)
                beta = jnp.where(live, beta, 0.0)
            gc = _dot(incl.astype(F32), g, HIGHEST)
            g_last = gc[c - 1:c, :]
            beta_all[sj] = beta
            gc_all[sj] = gc
            gc_t[sj] = _dot_nt(eye_l, gc, HIGHEST)
            gam_all[sj] = jnp.exp(gc)
            kdec_all[sj] = jnp.exp(g_last - gc)
            gend_all[sj] = jnp.exp(g_last)

        q, k, v, beta, gam, decay = {}, {}, {}, {}, {}, {}
        for un in units:
            s, j, h = un
            lo = h * HEAD_DIM
            qh = conv_slab(s, j, lo)
            kh = conv_slab(s, j, tok_w + lo)
            v[un] = conv_slab(s, j, 2 * tok_w + lo)
            q[un] = qh * lax.rsqrt(jnp.sum(qh * qh, axis=-1, keepdims=True) + EPS) * (HEAD_DIM ** -0.5)
            k[un] = kh * lax.rsqrt(jnp.sum(kh * kh, axis=-1, keepdims=True) + EPS)
            beta[un] = beta_all[s, j][:, n_heads + h:n_heads + h + 1]
            gam[un] = gam_all[s, j][:, h:h + 1]
            diff = gc_all[s, j][:, h:h + 1] - gc_t[s, j][h:h + 1, :]
            decay[un] = jnp.where(incl, jnp.exp(jnp.where(incl, diff, 0.0)), 0.0)

        k_b = {un: k[un].astype(BF16) for un in units}
        qk_kk = {un: _dot_nt(jnp.concatenate([q[un].astype(BF16), k_b[un]], axis=0), k_b[un]) for un in units}

        x_pow = {un: -(jnp.where(strict, qk_kk[un][c:] * decay[un], 0.0) * beta[un]) for un in units}
        t_off = dict(x_pow)
        for lvl in range(1, n_levels + 1):
            for un in units:
                x_b = x_pow[un].astype(BF16)
                if lvl == 1:
                    x_pow[un] = _dot(x_b, x_b)
                elif lvl < n_levels:
                    both = _dot(jnp.concatenate([t_off[un].astype(BF16), x_b], axis=0), x_b)
                    t_off[un] = t_off[un] + x_pow[un] + both[:c]
                    x_pow[un] = both[c:]
                else:
                    t_off[un] = t_off[un] + x_pow[un] + _dot(t_off[un].astype(BF16), x_b)

        sol = {}
        for un in units:
            rhs = jnp.concatenate([v[un] * beta[un], k[un] * (beta[un] * gam[un])], axis=1)
            sol[un] = rhs + _bdot(t_off[un], rhs)

        state = {(s, h): s_ref[s, h] for s in seqs for h in range(n_heads)}
        o = {}
        for j in range(n_sub):
            now = [(s, j, h) for s in seqs for h in range(n_heads)]
            from_state = {un: _bdot(jnp.concatenate([sol[un][:, HEAD_DIM:], q[un] * gam[un]], axis=0),
                                    state[un[0], un[2]]) for un in now}
            u = {un: sol[un][:, :HEAD_DIM] - from_state[un][:c] for un in now}
            for un in now:
                o[un] = from_state[un][c:] + _bdot(jnp.where(incl, qk_kk[un][:c] * decay[un], 0.0), u[un])
            for un in now:
                s, _, h = un
                state[s, h] = (state[s, h] * gend_all[s, j][:, h:h + 1]
                               + _bdot_tn(k[un] * kdec_all[s, j][:, h:h + 1], u[un]))
        for (s, h), val in state.items():
            s_ref[s, h] = val

        for un in units:
            s, j, h = un
            lo = h * HEAD_DIM
            oh = o[un] * lax.rsqrt(jnp.mean(o[un] * o[un], axis=-1, keepdims=True) + EPS) * onw_ref[...]
            out = oh * _silu(chunk_cols(s, j, qkv_w + lo, qkv_w + lo + HEAD_DIM))
            if padded:
                y_scr[s * c_real:(s + 1) * c_real, lo:lo + HEAD_DIM] = out[:c_real]
            else:
                r0 = s * seq_rows + j * c
                y_ref[r0:r0 + c, lo:lo + HEAD_DIM] = out.astype(y_ref.dtype)

    if padded:
        y_ref[...] = y_scr[...].astype(y_ref.dtype)
    for s in range(bb):
        if padded:
            carry = xp_scr[s, CARRY_OFF + c_real:SUBLANE + c_real, 0:qkv_w]
        else:
            carry = main_ref[(s + 1) * seq_rows - (CONV_W - 1):(s + 1) * seq_rows, 0:qkv_w]
        xp_scr[s, CARRY_OFF:SUBLANE, 0:qkv_w] = carry
        if seq_minor:
            for i in range(CONV_W - 1):
                cnew_ref[i, s:s + 1, :] = carry[i:i + 1, :]
        else:
            cnew_ref[s] = carry


def delta_mixer(main, small, row_off, conv_buf, conv_layer, conv_out_prev, s0, s_layer, s_out_prev, y_prev, y_shape,
                conv_w, a_log, dt_bias, onorm_w, *, batch, seq, chunk, bb, n_sub, seq_group, seq_minor):
    n_layers, _, n_heads = s0.shape[:3]
    tok_w = n_heads * HEAD_DIM
    main_w = 4 * tok_w
    qkv_w = 3 * tok_w
    c_real = min(chunk, seq)
    c_pad = max(c_real, SUBLANE)
    assert bb == 1 or c_real == seq
    nch = seq // (n_sub * c_real)
    rows = bb * n_sub * c_real
    blk0 = row_off // rows
    row_spec = lambda w: pl.BlockSpec((rows, w), lambda b, ch: (blk0 + b * nch + ch, 0))
    pad_lane = lambda p: jnp.zeros((1, LANE), F32).at[0, :n_heads].set(p.astype(F32))
    s_spec = pl.BlockSpec((None, bb, n_heads, HEAD_DIM, HEAD_DIM), lambda b, ch: (s_layer, b, 0, 0, 0))
    if seq_minor:
        conv_in_spec = pl.BlockSpec((None, CONV_W - 1, bb, qkv_w), lambda b, ch: (conv_layer, 0, b, 0))
        conv_out_spec = conv_in_spec
        conv_out_shape = (conv_buf.shape[0], CONV_W - 1, batch, qkv_w)
    else:
        conv_in_spec = pl.BlockSpec((None, bb, CONV_W - 1, qkv_w), lambda b, ch: (conv_layer, b, 0, 0))
        conv_out_spec = pl.BlockSpec((bb, CONV_W - 1, qkv_w), lambda b, ch: (b, 0, 0))
        conv_out_shape = (batch, CONV_W - 1, qkv_w)
    in_specs = [
        row_spec(main_w),
        row_spec(small.shape[-1]),
        conv_in_spec,
        s_spec,
        pl.BlockSpec((CONV_W, qkv_w), lambda b, ch: (0, 0)),
        pl.BlockSpec((1, LANE), lambda b, ch: (0, 0)),
        pl.BlockSpec((1, LANE), lambda b, ch: (0, 0)),
        pl.BlockSpec((1, HEAD_DIM), lambda b, ch: (0, 0)),
    ]
    args = [main, small, conv_buf, s0, conv_w, pad_lane(a_log), pad_lane(dt_bias), onorm_w.reshape(1, HEAD_DIM)]
    aliases = {}
    aliases.update(_alias_args(in_specs, args, y_prev, 0))
    aliases.update(_alias_args(in_specs, args, conv_out_prev, 1))
    aliases.update(_alias_args(in_specs, args, s_out_prev, 2))
    body = functools.partial(_delta_body, c_real=c_real, c_pad=c_pad, bb=bb, n_sub=n_sub, n_heads=n_heads,
                             tok_w=tok_w, seq_group=seq_group, n_alias=len(aliases), seq_minor=seq_minor)
    return pl.pallas_call(
        body,
        grid=(batch // bb, nch),
        in_specs=in_specs,
        out_specs=[
            row_spec(tok_w),
            conv_out_spec,
            s_spec,
        ],
        out_shape=[
            jax.ShapeDtypeStruct(y_shape, BF16),
            jax.ShapeDtypeStruct(conv_out_shape, F32),
            jax.ShapeDtypeStruct((n_layers, batch, n_heads, HEAD_DIM, HEAD_DIM), F32),
        ],
        scratch_shapes=[pltpu.VMEM((bb, SUBLANE + c_pad, main_w), F32), pltpu.VMEM((bb, c_pad, LANE), F32),
                        pltpu.VMEM((bb * c_real, tok_w), F32)],
        input_output_aliases=aliases,
        compiler_params=_params("arbitrary", "arbitrary"),
        name="delta_mixer",
    )(*args)


def _pool_body(*refs, c_real, c_pad, bb, n_past, tok_w, n_alias, seq_minor):
    u_ref, buf_ref, wg_ref, scale_ref = refs[:4]
    y_ref, pnew_ref, hist_scr, y_scr = refs[4 + n_alias:]
    c = c_pad
    padded = c_pad > c_real
    gw = tok_w // len(POOL_WINDOWS)

    @pl.when(pl.program_id(1) == 0)
    def _init():
        for s in range(bb):
            hist_scr[s, 0:1, :] = jnp.zeros((1, tok_w), F32)
            if seq_minor:
                for j in range(POOL_HIST - 1):
                    hist_scr[s, 1 + j:2 + j, :] = buf_ref[j, s:s + 1, :]
            else:
                hist_scr[s, 1:POOL_HIST, :] = buf_ref[s]
            if padded:
                hist_scr[s, POOL_HIST + c_real:POOL_HIST + c, :] = jnp.zeros((c - c_real, tok_w), F32)

    for s in range(bb):
        hist_scr[s, POOL_HIST:POOL_HIST + c_real, :] = u_ref[s * c_real:(s + 1) * c_real, :]
    pos = pl.program_id(1) * c_real + lax.broadcasted_iota(jnp.int32, (c, 1), 0)
    for gi, win in enumerate(POOL_WINDOWS):
        lo = gi * gw
        inv_cnt = 1.0 / jnp.minimum(win, pos + 1 + n_past).astype(F32)
        ds = []
        for s in range(bb):
            tok = hist_scr[s, POOL_HIST:POOL_HIST + c, lo:lo + gw]
            acc = tok
            for back in range(1, win):
                acc = acc + hist_scr[s, POOL_HIST - back:POOL_HIST - back + c, lo:lo + gw]
            ds.append(acc * inv_cnt - tok)
        d_all = ds[0] if bb == 1 else jnp.concatenate(ds, axis=0)
        y_all = _bdot(d_all, wg_ref[gi]) * scale_ref[:, lo:lo + gw]
        if padded:
            for s in range(bb):
                y_scr[s * c_real:(s + 1) * c_real, lo:lo + gw] = y_all[s * c:s * c + c_real]
        else:
            y_ref[:, lo:lo + gw] = y_all.astype(y_ref.dtype)

    if padded:
        y_ref[...] = y_scr[...].astype(y_ref.dtype)
    for s in range(bb):
        tail = hist_scr[s, c_real + 1:c_real + POOL_HIST, :]
        hist_scr[s, 1:POOL_HIST, :] = tail
        if seq_minor:
            for j in range(POOL_HIST - 1):
                pnew_ref[j, s:s + 1, :] = tail[j:j + 1, :]
        else:
            pnew_ref[s] = tail


def pool_mixer(u, row_off, buf, buf_layer, buf_out_prev, w_grp, scale, layer, y_prev, y_shape, *,
               batch, seq, chunk, bb, n_past, seq_minor):
    tok_w = buf.shape[-1]
    gw = tok_w // len(POOL_WINDOWS)
    c_real = min(chunk, seq)
    c_pad = max(c_real, SUBLANE)
    assert bb == 1 or c_real == seq
    nch = seq // c_real
    rows = bb * c_real
    blk0 = row_off // rows
    row_spec = pl.BlockSpec((rows, tok_w), lambda b, ch: (blk0 + b * nch + ch, 0))
    if seq_minor:
        buf_in_spec = pl.BlockSpec((None, POOL_HIST - 1, bb, tok_w), lambda b, ch: (buf_layer, 0, b, 0))
        buf_out_spec = buf_in_spec
        buf_out_shape = (buf.shape[0], POOL_HIST - 1, batch, tok_w)
    else:
        buf_in_spec = pl.BlockSpec((None, bb, POOL_HIST - 1, tok_w), lambda b, ch: (buf_layer, b, 0, 0))
        buf_out_spec = pl.BlockSpec((bb, POOL_HIST - 1, tok_w), lambda b, ch: (b, 0, 0))
        buf_out_shape = (batch, POOL_HIST - 1, tok_w)
    in_specs = [
        row_spec,
        buf_in_spec,
        pl.BlockSpec((None, len(POOL_WINDOWS), gw, gw), lambda b, ch: (layer, 0, 0, 0)),
        pl.BlockSpec((None, 1, tok_w), lambda b, ch: (layer, 0, 0)),
    ]
    args = [u, buf, w_grp, scale.reshape(scale.shape[0], 1, tok_w)]
    aliases = _alias_args(in_specs, args, y_prev, 0)
    aliases.update(_alias_args(in_specs, args, buf_out_prev, 1))
    body = functools.partial(_pool_body, c_real=c_real, c_pad=c_pad, bb=bb, n_past=n_past, tok_w=tok_w,
                             n_alias=len(aliases), seq_minor=seq_minor)
    return pl.pallas_call(
        body,
        grid=(batch // bb, nch),
        in_specs=in_specs,
        out_specs=[row_spec, buf_out_spec],
        out_shape=[jax.ShapeDtypeStruct(y_shape, BF16), jax.ShapeDtypeStruct(buf_out_shape, F32)],
        scratch_shapes=[pltpu.VMEM((bb, POOL_HIST + c_pad, tok_w), F32), pltpu.VMEM((rows, tok_w), F32)],
        input_output_aliases=aliases,
        compiler_params=_params("arbitrary", "arbitrary"),
        name="pool_mixer",
    )(*args)


def _softmax_rows(s):
    p = jnp.exp(s - jnp.max(s, axis=-1, keepdims=True))
    return p / jnp.sum(p, axis=-1, keepdims=True)


def _xattn_prompt_body(q_ref, k_ref, v_ref, y_prev_ref, o_ref):
    del y_prev_ref
    for h in range(N_MEM_HEADS):
        lo = h * HEAD_DIM
        s = _bdot_nt(q_ref[:, lo:lo + HEAD_DIM], k_ref[:, lo:lo + HEAD_DIM]) * (HEAD_DIM ** -0.5)
        o = _bdot(_softmax_rows(s), v_ref[:, lo:lo + HEAD_DIM])
        o_ref[:, lo:lo + HEAD_DIM] = o.astype(o_ref.dtype)


def cross_attn_prompt(q, q_col_block, kv, y_prev, *, batch, seq, tq):
    n_mem = kv.shape[1]
    tq = min(tq, seq)
    nq = seq // tq
    y_col_block = y_prev.shape[1] // MEM_W - 1
    return pl.pallas_call(
        _xattn_prompt_body,
        grid=(batch, nq),
        in_specs=[
            pl.BlockSpec((tq, MEM_W), lambda b, i: (b * nq + i, q_col_block)),
            pl.BlockSpec((None, n_mem, MEM_W), lambda b, i: (b, 0, 0)),
            pl.BlockSpec((None, n_mem, MEM_W), lambda b, i: (b, 0, 1)),
            pl.BlockSpec(memory_space=pl.ANY),
        ],
        out_specs=pl.BlockSpec((tq, MEM_W), lambda b, i: (b * nq + i, y_col_block)),
        out_shape=jax.ShapeDtypeStruct(y_prev.shape, y_prev.dtype),
        input_output_aliases={3: 0},
        compiler_params=_params("arbitrary", "arbitrary"),
        name="cross_attn_prompt",
    )(q, kv, kv, y_prev)


def _xattn_sample_body(q_ref, k_ref, v_ref, y_prev_ref, o_ref, q_scr, o_scr, *, bb, seq):
    del y_prev_ref
    n_rows = k_ref.shape[1]
    q_scr[...] = jnp.zeros(q_scr.shape, F32)
    for b in range(bb):
        q_scr[b, 0:seq, :] = q_ref[b * seq:(b + 1) * seq, :]
    shape = (N_MEM_HEADS * SUBLANE, n_rows)
    same_head = (lax.broadcasted_iota(jnp.int32, shape, 1) % N_MEM_HEADS
                 == lax.broadcasted_iota(jnp.int32, shape, 0) // SUBLANE)
    scores = []
    for b in range(bb):
        qb = q_scr[b]
        q_rows = jnp.concatenate([qb[:, h * HEAD_DIM:(h + 1) * HEAD_DIM] for h in range(N_MEM_HEADS)], axis=0)
        scores.append(_bdot_nt(q_rows, k_ref[b]) * (HEAD_DIM ** -0.5))
    probs = []
    for s in scores:
        p = jnp.exp(s - jnp.max(jnp.where(same_head, s, -jnp.inf), axis=-1, keepdims=True))
        p = jnp.where(same_head, p, 0.0)
        probs.append(p / jnp.sum(p, axis=-1, keepdims=True))
    outs = [_bdot(p, v_ref[b]) for b, p in enumerate(probs)]
    for b, o in enumerate(outs):
        for h in range(N_MEM_HEADS):
            o_scr[b * seq:(b + 1) * seq, h * HEAD_DIM:(h + 1) * HEAD_DIM] = o[h * SUBLANE:h * SUBLANE + seq]
    o_ref[...] = o_scr[...].astype(o_ref.dtype)


def cross_attn_sample(q, q_col_block, row_off, cache_k, cache_v, layer, y_prev, *, batch, seq, bb):
    n_rows = cache_k.shape[2]
    rows = bb * seq
    blk0 = row_off // rows
    y_col_block = y_prev.shape[1] // MEM_W - 1
    kv_spec = pl.BlockSpec((None, bb, n_rows, HEAD_DIM), lambda i: (layer, i, 0, 0))
    return pl.pallas_call(
        functools.partial(_xattn_sample_body, bb=bb, seq=seq),
        grid=(batch // bb,),
        in_specs=[pl.BlockSpec((rows, MEM_W), lambda i: (blk0 + i, q_col_block)), kv_spec, kv_spec,
                  pl.BlockSpec(memory_space=pl.ANY)],
        out_specs=pl.BlockSpec((rows, MEM_W), lambda i: (blk0 + i, y_col_block)),
        out_shape=jax.ShapeDtypeStruct(y_prev.shape, y_prev.dtype),
        scratch_shapes=[pltpu.VMEM((bb, SUBLANE, MEM_W), F32), pltpu.VMEM((rows, MEM_W), F32)],
        input_output_aliases={3: 0},
        compiler_params=_params("arbitrary"),
        name="cross_attn_sample",
    )(q, cache_k, cache_v, y_prev)


def kernel(x_prompt, x_sample, mem_prompt, state_delta_S, state_delta_conv, state_pool, cache_mem_k, cache_mem_v,
           norm_mix, norm_ffn, norm_mem, norm_final, w_in_delta, conv_w, a_log, dt_bias, delta_onorm,
           w_in_pool, w_pool_grp, pool_scale, w_mem_kv, w_out, w_gate_up, w_down):
    bp, lp, d = x_prompt.shape
    bs, ls, _ = x_sample.shape
    depth = w_out.shape[0]
    n_mem = mem_prompt.shape[1]
    n_delta, _, n_heads = state_delta_S.shape[:3]
    tok_w = n_heads * HEAD_DIM
    main_w = 4 * tok_w
    d_ff = w_down.shape[1]
    tp, ts = bp * lp, bs * ls
    n_past = min(PAST_LEN, POOL_HIST - 1)
    y_shape = (tp + ts, tok_w + MEM_W)

    x = (x_prompt.reshape(tp, d), x_sample.reshape(ts, d))

    mem_rows = mem_prompt.reshape(bp * n_mem, d)
    p_kv = [matmul(rmsnorm(mem_rows, norm_mem[l], BF16), w_mem_kv, l, ncols=2 * MEM_W, tn=MEM_W, out_dtype=F32)
            for l in range(depth)]
    p_mem_k = jnp.stack([kv[:, :MEM_W].reshape(bp, n_mem, N_MEM_HEADS, HEAD_DIM) for kv in p_kv])
    p_mem_v = jnp.stack([kv[:, MEM_W:].reshape(bp, n_mem, N_MEM_HEADS, HEAD_DIM) for kv in p_kv])
    cache_k = cache_mem_k.reshape(depth, bs, n_mem * N_MEM_HEADS, HEAD_DIM)
    cache_v = cache_mem_v.reshape(depth, bs, n_mem * N_MEM_HEADS, HEAD_DIM)

    w_in_delta_t = jnp.swapaxes(w_in_delta, 1, 2)
    zero_s = jnp.zeros((n_delta, bp, n_heads, HEAD_DIM, HEAD_DIM), F32)
    zero_conv = jnp.zeros((1, bp, CONV_W - 1, 3 * tok_w), F32)
    zero_pool = jnp.zeros((1, bp, POOL_HIST - 1, tok_w), F32)
    conv_in = jnp.swapaxes(state_delta_conv, 1, 2)
    pool_in = jnp.swapaxes(state_pool, 1, 2)

    p_conv, p_pool = [], []
    p_s = s_s = s_conv = s_pool = None
    di = pi = 0
    for l in range(depth):
        h = rmsnorm_concat(*x, norm_mix[l], BF16) if l == 0 else rmsnorm(x, norm_mix[l], BF16)
        if l % 2 == 0:
            proj = matmul(h, w_in_delta_t, di, ncols=main_w, tn=1024, out_dtype=F32, w_transposed=True)
            tail0 = main_w // LANE
            small = matmul(h, w_in_delta_t, di, ncols=MEM_W + LANE, tn=LANE, out_dtype=F32, w_transposed=True,
                           col_blocks=tuple(range(tail0, tail0 + (MEM_W + LANE) // LANE)),
                           out_slices=((2 * n_heads, 2 * n_heads + MEM_W), (0, LANE)))
            gates = (conv_w[di], a_log[di], dt_bias[di], delta_onorm[di])
            y, conv_p, p_s = delta_mixer(proj, small, 0, zero_conv, 0, None, zero_s, di, p_s, None, y_shape, *gates,
                                         batch=bp, seq=lp, chunk=64, bb=1, n_sub=4, seq_group=1, seq_minor=False)
            y, s_conv, s_s = delta_mixer(proj, small, tp, conv_in, di, s_conv, state_delta_S, di, s_s, y, y_shape,
                                         *gates, batch=bs, seq=ls, chunk=64, bb=8, n_sub=1, seq_group=4,
                                         seq_minor=True)
            p_conv.append(conv_p)
            q_src, q_blk = small, 0
            di += 1
        else:
            proj = matmul(h, w_in_pool, pi, ncols=tok_w + MEM_W, tn=1024, out_dtype=F32)
            y, pool_p = pool_mixer(proj, 0, zero_pool, 0, None, w_pool_grp, pool_scale, pi, None, y_shape,
                                   batch=bp, seq=lp, chunk=256, bb=1, n_past=0, seq_minor=False)
            y, s_pool = pool_mixer(proj, tp, pool_in, pi, s_pool, w_pool_grp, pool_scale, pi, y, y_shape,
                                   batch=bs, seq=ls, chunk=128, bb=16, n_past=n_past, seq_minor=True)
            p_pool.append(pool_p)
            q_src, q_blk = proj, tok_w // MEM_W
            pi += 1

        y = cross_attn_prompt(q_src, q_blk, p_kv[l].reshape(bp, n_mem, 2 * MEM_W), y, batch=bp, seq=lp, tq=1024)
        y = cross_attn_sample(q_src, q_blk, tp, cache_k, cache_v, l, y, batch=bs, seq=ls, bb=8)

        row_tiles = (256, 128) if l == 0 else (544, 512, 256, 128)
        fits = (lambda c: tp % c == 0 and ts % c == 0) if l == 0 else (lambda c: (tp + ts) % c == 0)
        x, h2 = matmul_residual_norm(y, w_out, l, x, norm_ffn[l], tm=next(c for c in row_tiles if fits(c)))
        act = matmul(h2, w_gate_up, l, ncols=d_ff, tn=512, out_dtype=BF16, col_blocks=(0, d_ff // 512), swiglu=True,
                     tm_max=2176)
        x = matmul(act, w_down, l, ncols=d, tn=1024, out_dtype=F32, residual=x, tm_max=256, w_double_buffer=False)

    y_p, y_s = rmsnorm_split(x, norm_final, tp)
    return (y_p.reshape(bp, lp, d), y_s.reshape(bs, ls, d),
            p_s, jnp.stack(p_conv), jnp.stack(p_pool), p_mem_k, p_mem_v,
            s_s, jnp.swapaxes(s_conv, 1, 2), jnp.swapaxes(s_pool, 1, 2))
```

```python
import functools
import math

import jax
import jax.numpy as jnp
from jax import lax
from jax.experimental import pallas as pl
from jax.experimental.pallas import tpu as pltpu

F32 = jnp.float32
BF16 = jnp.bfloat16
HIGHEST = lax.Precision.HIGHEST

EPS = 1e-6
HEAD_DIM = 128
N_MEM_HEADS = 4
MEM_W = N_MEM_HEADS * HEAD_DIM
CONV_W = 4
POOL_WINDOWS = (2, 4, 8, 16)
POOL_HIST = max(POOL_WINDOWS)
PAST_LEN = 16384
LANE = 128
SUBLANE = 8
MXU_COLS = 256
CARRY_OFF = SUBLANE - (CONV_W - 1)
VMEM_LIMIT = 56 * 1024 * 1024


def _params(*sem):
    return pltpu.CompilerParams(dimension_semantics=sem, vmem_limit_bytes=VMEM_LIMIT)


def _sigmoid(x):
    return 1.0 / (1.0 + jnp.exp(-x))


def _silu(x):
    half = 0.5 * x
    return half + half * jnp.tanh(half)


def _dot(a, b, precision=None):
    return jnp.dot(a, b, preferred_element_type=F32, precision=precision)


def _dot_nt(a, b, precision=None):
    return lax.dot_general(a, b, (((1,), (1,)), ((), ())), preferred_element_type=F32, precision=precision)


def _dot_tn(a, b, precision=None):
    return lax.dot_general(a, b, (((0,), (0,)), ((), ())), preferred_element_type=F32, precision=precision)


def _bdot(a, b):
    return _dot(a.astype(BF16), b.astype(BF16))


def _bdot_nt(a, b):
    return _dot_nt(a.astype(BF16), b.astype(BF16))


def _bdot_tn(a, b):
    return _dot_tn(a.astype(BF16), b.astype(BF16))


def _alias_args(in_specs, args, prev, out_index):
    if prev is None:
        return {}
    in_specs.append(pl.BlockSpec(memory_space=pl.ANY))
    args.append(prev)
    return {len(args) - 1: out_index}


def _rmsnorm_body(x_ref, w_ref, o_ref):
    x = x_ref[...]
    ms = jnp.mean(x * x, axis=-1, keepdims=True)
    o_ref[...] = (x * lax.rsqrt(ms + EPS) * w_ref[...]).astype(o_ref.dtype)


def rmsnorm(x, w, out_dtype):
    t, d = x.shape
    tm = next(c for c in (1088, 1024, 512, 256, 128) if t % c == 0)
    return pl.pallas_call(
        _rmsnorm_body,
        grid=(t // tm,),
        in_specs=[pl.BlockSpec((tm, d), lambda i: (i, 0)), pl.BlockSpec((1, d), lambda i: (0, 0))],
        out_specs=pl.BlockSpec((tm, d), lambda i: (i, 0)),
        out_shape=jax.ShapeDtypeStruct((t, d), out_dtype),
        compiler_params=_params("arbitrary"),
        name="rmsnorm",
    )(x, w.reshape(1, d))


def _rmsnorm_concat_body(a_ref, b_ref, w_ref, o_ref, *, n_a):
    x = jnp.where(pl.program_id(0) < n_a, a_ref[...], b_ref[...])
    ms = jnp.mean(x * x, axis=-1, keepdims=True)
    o_ref[...] = (x * lax.rsqrt(ms + EPS) * w_ref[...]).astype(o_ref.dtype)


def rmsnorm_concat(xa, xb, w, out_dtype):
    (ra, d), rb = xa.shape, xb.shape[0]
    tm = next(c for c in (512, 256, 128) if ra % c == 0 and rb % c == 0)
    n_a = ra // tm
    return pl.pallas_call(
        functools.partial(_rmsnorm_concat_body, n_a=n_a),
        grid=((ra + rb) // tm,),
        in_specs=[pl.BlockSpec((tm, d), lambda i: (jnp.minimum(i, n_a - 1), 0)),
                  pl.BlockSpec((tm, d), lambda i: (jnp.maximum(i - n_a, 0), 0)),
                  pl.BlockSpec((1, d), lambda i: (0, 0))],
        out_specs=pl.BlockSpec((tm, d), lambda i: (i, 0)),
        out_shape=jax.ShapeDtypeStruct((ra + rb, d), out_dtype),
        compiler_params=_params("arbitrary"),
        name="rmsnorm_concat",
    )(xa, xb, w.reshape(1, d))


def _rmsnorm_split_body(x_ref, w_ref, a_ref, b_ref, *, n_a):
    x = x_ref[...]
    ms = jnp.mean(x * x, axis=-1, keepdims=True)
    y = x * lax.rsqrt(ms + EPS) * w_ref[...]

    @pl.when(pl.program_id(0) < n_a)
    def _first():
        a_ref[...] = y

    @pl.when(pl.program_id(0) >= n_a)
    def _second():
        b_ref[...] = y


def rmsnorm_split(x, w, rows_a):
    t, d = x.shape
    tm = next(c for c in (512, 256, 128) if rows_a % c == 0 and (t - rows_a) % c == 0)
    n_a = rows_a // tm
    return pl.pallas_call(
        functools.partial(_rmsnorm_split_body, n_a=n_a),
        grid=(t // tm,),
        in_specs=[pl.BlockSpec((tm, d), lambda i: (i, 0)), pl.BlockSpec((1, d), lambda i: (0, 0))],
        out_specs=[pl.BlockSpec((tm, d), lambda i: (jnp.minimum(i, n_a - 1), 0)),
                   pl.BlockSpec((tm, d), lambda i: (jnp.maximum(i - n_a, 0), 0))],
        out_shape=[jax.ShapeDtypeStruct((rows_a, d), F32), jax.ShapeDtypeStruct((t - rows_a, d), F32)],
        compiler_params=_params("arbitrary"),
        name="rmsnorm_split",
    )(x, w.reshape(1, d))


def _mm_body(*refs, n_w, tn, has_res, swiglu, out_slices, slab_valid, w_transposed):
    x_ref = refs[0]
    w_refs = refs[1:1 + n_w]
    res_ref = refs[1 + n_w] if has_res else None
    o_ref = refs[-2]
    wb_ref = refs[-1]

    @pl.when(pl.program_id(1) == 0)
    def _cast_weights():
        for i, w_ref in enumerate(w_refs):
            valid = slab_valid[i]
            if w_transposed:
                wb_ref[i * tn:i * tn + valid, :] = w_ref[0:valid, :].astype(BF16)
                if valid < tn:
                    wb_ref[i * tn + valid:(i + 1) * tn, :] = jnp.zeros((tn - valid, wb_ref.shape[1]), BF16)
            else:
                wb_ref[:, i * tn:i * tn + valid] = w_ref[:, 0:valid].astype(BF16)
                if valid < tn:
                    wb_ref[:, i * tn + valid:(i + 1) * tn] = jnp.zeros((wb_ref.shape[0], tn - valid), BF16)

    def product(lo, hi):
        if w_transposed:
            return _dot_nt(x_ref[...], wb_ref[lo:hi, :])
        return _dot(x_ref[...], wb_ref[:, lo:hi])

    if out_slices is not None:
        acc = product(0, n_w * tn)
        acc = jnp.concatenate([acc[:, lo:hi] for lo, hi in out_slices], axis=1)
        o_ref[...] = acc.astype(o_ref.dtype)
        return
    for c0 in range(0, tn, MXU_COLS):
        c1 = min(c0 + MXU_COLS, tn)
        acc = product(c0, c1)
        if swiglu:
            acc = _silu(acc) * product(tn + c0, tn + c1)
        if has_res:
            acc = acc + res_ref[:, c0:c1]
        o_ref[:, c0:c1] = acc.astype(o_ref.dtype)


def matmul(x, w, layer, *, ncols, tn, out_dtype, col_blocks=(0,), residual=None, swiglu=False, out_slices=None,
           w_transposed=False, tm_max=1088):
    t, k = x.shape
    tm = next(c for c in (2176, 1088, 1024, 512, 256, 128) if c <= tm_max and t % c == 0)
    n_w = len(col_blocks)
    to = tn if out_slices is None else sum(hi - lo for lo, hi in out_slices)
    in_specs = [pl.BlockSpec((tm, k), lambda j, m: (m, 0))]
    for cb in col_blocks:
        if w_transposed:
            in_specs.append(pl.BlockSpec((None, tn, k), lambda j, m, cb=cb: (layer, cb + j, 0)))
        else:
            in_specs.append(pl.BlockSpec((None, k, tn), lambda j, m, cb=cb: (layer, 0, cb + j)))
    args = [x] + [w] * n_w
    if residual is not None:
        in_specs.append(pl.BlockSpec((tm, to), lambda j, m: (m, j)))
        args.append(residual)
    n_tiles = ncols // to
    n_total = w.shape[1] if w_transposed else w.shape[2]
    slab_valid = tuple(min(tn, n_total - (cb + n_tiles - 1) * tn) for cb in col_blocks)
    assert all(v == tn for v in slab_valid) or n_tiles == 1
    body = functools.partial(_mm_body, n_w=n_w, tn=tn, has_res=residual is not None, swiglu=swiglu,
                             out_slices=out_slices, slab_valid=slab_valid, w_transposed=w_transposed)
    return pl.pallas_call(
        body,
        grid=(n_tiles, t // tm),
        in_specs=in_specs,
        out_specs=pl.BlockSpec((tm, to), lambda j, m: (m, j)),
        out_shape=jax.ShapeDtypeStruct((t, ncols), out_dtype),
        scratch_shapes=[pltpu.VMEM((n_w * tn, k) if w_transposed else (k, n_w * tn), BF16)],
        compiler_params=_params("arbitrary", "arbitrary"),
        name="matmul",
    )(*args)


def _proj_norm_body(*refs, n_a):
    x_ref, w_ref, res_ref = refs[:3]
    res_b_ref = refs[3] if n_a is not None else None
    nw_ref, o_ref, h_ref, wb_ref = refs[-4:]

    def residual(c0):
        res = res_ref[:, c0:c0 + MXU_COLS]
        if res_b_ref is None:
            return res
        return jnp.where(pl.program_id(0) < n_a, res, res_b_ref[:, c0:c0 + MXU_COLS])

    @pl.when(pl.program_id(0) == 0)
    def _cast_weights():
        wb_ref[...] = w_ref[...].astype(BF16)

    n = o_ref.shape[1]
    sumsq = jnp.zeros((o_ref.shape[0], LANE), F32)
    for c0 in range(0, n, MXU_COLS):
        acc = _dot(x_ref[...], wb_ref[:, c0:c0 + MXU_COLS]) + residual(c0)
        o_ref[:, c0:c0 + MXU_COLS] = acc
        sq = acc * acc
        for l0 in range(0, MXU_COLS, LANE):
            sumsq = sumsq + sq[:, l0:l0 + LANE]
    scale = lax.rsqrt(jnp.sum(sumsq, axis=-1, keepdims=True) / n + EPS)
    for c0 in range(0, n, MXU_COLS):
        h = o_ref[:, c0:c0 + MXU_COLS] * scale * nw_ref[:, c0:c0 + MXU_COLS]
        h_ref[:, c0:c0 + MXU_COLS] = h.astype(h_ref.dtype)


def matmul_residual_norm(x, w, layer, residual, norm_w, *, tm):
    t, k = x.shape
    n = w.shape[2]
    in_specs = [
        pl.BlockSpec((tm, k), lambda m: (m, 0)),
        pl.BlockSpec((None, k, n), lambda m: (layer, 0, 0), pipeline_mode=pl.Buffered(1)),
    ]
    if isinstance(residual, tuple):
        n_a = residual[0].shape[0] // tm
        in_specs += [pl.BlockSpec((tm, n), lambda m: (jnp.minimum(m, n_a - 1), 0)),
                     pl.BlockSpec((tm, n), lambda m: (jnp.maximum(m - n_a, 0), 0))]
    else:
        n_a, residual = None, (residual,)
        in_specs.append(pl.BlockSpec((tm, n), lambda m: (m, 0)))
    in_specs.append(pl.BlockSpec((1, n), lambda m: (0, 0)))
    return pl.pallas_call(
        functools.partial(_proj_norm_body, n_a=n_a),
        grid=(t // tm,),
        in_specs=in_specs,
        out_specs=[pl.BlockSpec((tm, n), lambda m: (m, 0)), pl.BlockSpec((tm, n), lambda m: (m, 0))],
        out_shape=[jax.ShapeDtypeStruct((t, n), F32), jax.ShapeDtypeStruct((t, n), BF16)],
        scratch_shapes=[pltpu.VMEM((k, n), BF16)],
        compiler_params=_params("arbitrary"),
        name="matmul_residual_norm",
    )(x, w, *residual, norm_w.reshape(1, n))


def _delta_body(*refs, c_real, c_pad, bb, n_sub, n_heads, tok_w, seq_group, n_alias, seq_minor):
    main_ref, small_ref, cbuf_ref, s0_ref, cw_ref, alog_ref, dtb_ref, onw_ref = refs[:8]
    y_ref, cnew_ref, s_ref, xp_scr, ab_scr, y_scr = refs[8 + n_alias:]
    qkv_w = 3 * tok_w
    c = c_pad
    padded = c_pad > c_real
    assert not (padded and n_sub > 1)
    seq_rows = n_sub * c_real

    @pl.when(pl.program_id(1) == 0)
    def _init():
        s_ref[...] = s0_ref[...]
        for s in range(bb):
            xp_scr[s, 0:CARRY_OFF, :] = jnp.zeros((CARRY_OFF, xp_scr.shape[2]), F32)
            if seq_minor:
                for i in range(CONV_W - 1):
                    xp_scr[s, CARRY_OFF + i:CARRY_OFF + i + 1, 0:qkv_w] = cbuf_ref[i, s:s + 1, :]
            else:
                xp_scr[s, CARRY_OFF:SUBLANE, 0:qkv_w] = cbuf_ref[s]
            if padded:
                xp_scr[s, SUBLANE + c_real:SUBLANE + c, :] = jnp.zeros((c - c_real, xp_scr.shape[2]), F32)
        if padded:
            ab_scr[...] = jnp.zeros(ab_scr.shape, F32)

    if padded:
        for s in range(bb):
            xp_scr[s, SUBLANE:SUBLANE + c_real, :] = main_ref[s * c_real:(s + 1) * c_real, :]
            ab_scr[s, 0:c_real, :] = small_ref[s * c_real:(s + 1) * c_real, MEM_W:MEM_W + LANE]

    def chunk_cols(s, j, lo, hi):
        if padded:
            return xp_scr[s, SUBLANE:SUBLANE + c, lo:hi]
        r0 = s * seq_rows + j * c
        return main_ref[r0:r0 + c, lo:hi]

    def conv_rows(s, j, col):
        if padded or j == 0:
            return jnp.concatenate([xp_scr[s, 0:SUBLANE, col:col + LANE], chunk_cols(s, 0, col, col + LANE)], axis=0)
        r0 = s * seq_rows + j * c
        return main_ref[r0 - SUBLANE:r0 + c, col:col + LANE]

    def conv_slab(s, j, col):
        blk = conv_rows(s, j, col)
        n = SUBLANE + c
        acc = blk[SUBLANE:] * cw_ref[CONV_W - 1:CONV_W, col:col + LANE]
        for i in range(CONV_W - 1):
            tap = pltpu.roll(blk, n - (CARRY_OFF + i), axis=0)[:c]
            acc = acc + tap * cw_ref[i:i + 1, col:col + LANE]
        return _silu(acc)

    rows = lax.broadcasted_iota(jnp.int32, (c, c), 0)
    cols = lax.broadcasted_iota(jnp.int32, (c, c), 1)
    incl = rows >= cols
    strict = rows > cols
    eye_l = (lax.broadcasted_iota(jnp.int32, (LANE, LANE), 0)
             == lax.broadcasted_iota(jnp.int32, (LANE, LANE), 1)).astype(F32)
    n_levels = int(math.log2(c))

    for s0 in range(0, bb, seq_group):
        seqs = range(s0, min(s0 + seq_group, bb))
        chunks = [(s, j) for s in seqs for j in range(n_sub)]
        units = [(s, j, h) for (s, j) in chunks for h in range(n_heads)]

        beta_all, gc_all, gc_t, gam_all, kdec_all, gend_all = {}, {}, {}, {}, {}, {}
        for sj in chunks:
            s, j = sj
            if padded:
                ab = ab_scr[s]
            else:
                r0 = s * seq_rows + j * c
                ab = small_ref[r0:r0 + c, MEM_W:MEM_W + LANE]
            a_in = ab + dtb_ref[...]
            softplus = jnp.maximum(a_in, 0.0) + jnp.log(1.0 + jnp.exp(-jnp.abs(a_in)))
            g = -jnp.exp(alog_ref[...]) * softplus
            beta = _sigmoid(ab)
            if padded:
                live = lax.broadcasted_iota(jnp.int32, (c, LANE), 0) < c_real
                g = jnp.where(live, g, 0.0)
                beta = jnp.where(live, beta, 0.0)
            gc = _dot(incl.astype(F32), g, HIGHEST)
            g_last = gc[c - 1:c, :]
            beta_all[sj] = beta
            gc_all[sj] = gc
            gc_t[sj] = _dot_nt(eye_l, gc, HIGHEST)
            gam_all[sj] = jnp.exp(gc)
            kdec_all[sj] = jnp.exp(g_last - gc)
            gend_all[sj] = jnp.exp(g_last)

        q, k, v, beta, gam, decay = {}, {}, {}, {}, {}, {}
        for un in units:
            s, j, h = un
            lo = h * HEAD_DIM
            qh = conv_slab(s, j, lo)
            kh = conv_slab(s, j, tok_w + lo)
            v[un] = conv_slab(s, j, 2 * tok_w + lo)
            q[un] = qh * lax.rsqrt(jnp.sum(qh * qh, axis=-1, keepdims=True) + EPS) * (HEAD_DIM ** -0.5)
            k[un] = kh * lax.rsqrt(jnp.sum(kh * kh, axis=-1, keepdims=True) + EPS)
            beta[un] = beta_all[s, j][:, n_heads + h:n_heads + h + 1]
            gam[un] = gam_all[s, j][:, h:h + 1]
            diff = gc_all[s, j][:, h:h + 1] - gc_t[s, j][h:h + 1, :]
            decay[un] = jnp.where(incl, jnp.exp(jnp.where(incl, diff, 0.0)), 0.0)

        k_b = {un: k[un].astype(BF16) for un in units}
        qk_kk = {un: _dot_nt(jnp.concatenate([q[un].astype(BF16), k_b[un]], axis=0), k_b[un]) for un in units}

        x_pow = {un: -(jnp.where(strict, qk_kk[un][c:] * decay[un], 0.0) * beta[un]) for un in units}
        t_off = dict(x_pow)
        for lvl in range(1, n_levels + 1):
            for un in units:
                x_b = x_pow[un].astype(BF16)
                if lvl == 1:
                    x_pow[un] = _dot(x_b, x_b)
                elif lvl < n_levels:
                    both = _dot(jnp.concatenate([t_off[un].astype(BF16), x_b], axis=0), x_b)
                    t_off[un] = t_off[un] + x_pow[un] + both[:c]
                    x_pow[un] = both[c:]
                else:
                    t_off[un] = t_off[un] + x_pow[un] + _dot(t_off[un].astype(BF16), x_b)

        sol = {}
        for un in units:
            rhs = jnp.concatenate([v[un] * beta[un], k[un] * (beta[un] * gam[un])], axis=1)
            sol[un] = rhs + _bdot(t_off[un], rhs)

        state = {(s, h): s_ref[s, h] for s in seqs for h in range(n_heads)}
        o = {}
        for j in range(n_sub):
            now = [(s, j, h) for s in seqs for h in range(n_heads)]
            from_state = {un: _bdot(jnp.concatenate([sol[un][:, HEAD_DIM:], q[un] * gam[un]], axis=0),
                                    state[un[0], un[2]]) for un in now}
            u = {un: sol[un][:, :HEAD_DIM] - from_state[un][:c] for un in now}
            for un in now:
                o[un] = from_state[un][c:] + _bdot(jnp.where(incl, qk_kk[un][:c] * decay[un], 0.0), u[un])
            for un in now:
                s, _, h = un
                state[s, h] = (state[s, h] * gend_all[s, j][:, h:h + 1]
                               + _bdot_tn(k[un] * kdec_all[s, j][:, h:h + 1], u[un]))
        for (s, h), val in state.items():
            s_ref[s, h] = val

        for un in units:
            s, j, h = un
            lo = h * HEAD_DIM
            oh = o[un] * lax.rsqrt(jnp.mean(o[un] * o[un], axis=-1, keepdims=True) + EPS) * onw_ref[...]
            out = oh * _silu(chunk_cols(s, j, qkv_w + lo, qkv_w + lo + HEAD_DIM))
            if padded:
                y_scr[s * c_real:(s + 1) * c_real, lo:lo + HEAD_DIM] = out[:c_real]
            else:
                r0 = s * seq_rows + j * c
                y_ref[r0:r0 + c, lo:lo + HEAD_DIM] = out.astype(y_ref.dtype)

    if padded:
        y_ref[...] = y_scr[...].astype(y_ref.dtype)
    for s in range(bb):
        if padded:
            carry = xp_scr[s, CARRY_OFF + c_real:SUBLANE + c_real, 0:qkv_w]
        else:
            carry = main_ref[(s + 1) * seq_rows - (CONV_W - 1):(s + 1) * seq_rows, 0:qkv_w]
        xp_scr[s, CARRY_OFF:SUBLANE, 0:qkv_w] = carry
        if seq_minor:
            for i in range(CONV_W - 1):
                cnew_ref[i, s:s + 1, :] = carry[i:i + 1, :]
        else:
            cnew_ref[s] = carry


def delta_mixer(main, small, row_off, conv_buf, conv_layer, conv_out_prev, s0, s_layer, s_out_prev, y_prev, y_shape,
                conv_w, a_log, dt_bias, onorm_w, *, batch, seq, chunk, bb, n_sub, seq_group, seq_minor):
    n_layers, _, n_heads = s0.shape[:3]
    tok_w = n_heads * HEAD_DIM
    main_w = 4 * tok_w
    qkv_w = 3 * tok_w
    c_real = min(chunk, seq)
    c_pad = max(c_real, SUBLANE)
    assert bb == 1 or c_real == seq
    nch = seq // (n_sub * c_real)
    rows = bb * n_sub * c_real
    blk0 = row_off // rows
    row_spec = lambda w: pl.BlockSpec((rows, w), lambda b, ch: (blk0 + b * nch + ch, 0))
    pad_lane = lambda p: jnp.zeros((1, LANE), F32).at[0, :n_heads].set(p.astype(F32))
    s_spec = pl.BlockSpec((None, bb, n_heads, HEAD_DIM, HEAD_DIM), lambda b, ch: (s_layer, b, 0, 0, 0))
    if seq_minor:
        conv_in_spec = pl.BlockSpec((None, CONV_W - 1, bb, qkv_w), lambda b, ch: (conv_layer, 0, b, 0))
        conv_out_spec = conv_in_spec
        conv_out_shape = (conv_buf.shape[0], CONV_W - 1, batch, qkv_w)
    else:
        conv_in_spec = pl.BlockSpec((None, bb, CONV_W - 1, qkv_w), lambda b, ch: (conv_layer, b, 0, 0))
        conv_out_spec = pl.BlockSpec((bb, CONV_W - 1, qkv_w), lambda b, ch: (b, 0, 0))
        conv_out_shape = (batch, CONV_W - 1, qkv_w)
    in_specs = [
        row_spec(main_w),
        row_spec(small.shape[-1]),
        conv_in_spec,
        s_spec,
        pl.BlockSpec((CONV_W, qkv_w), lambda b, ch: (0, 0)),
        pl.BlockSpec((1, LANE), lambda b, ch: (0, 0)),
        pl.BlockSpec((1, LANE), lambda b, ch: (0, 0)),
        pl.BlockSpec((1, HEAD_DIM), lambda b, ch: (0, 0)),
    ]
    args = [main, small, conv_buf, s0, conv_w, pad_lane(a_log), pad_lane(dt_bias), onorm_w.reshape(1, HEAD_DIM)]
    aliases = {}
    aliases.update(_alias_args(in_specs, args, y_prev, 0))
    aliases.update(_alias_args(in_specs, args, conv_out_prev, 1))
    aliases.update(_alias_args(in_specs, args, s_out_prev, 2))
    body = functools.partial(_delta_body, c_real=c_real, c_pad=c_pad, bb=bb, n_sub=n_sub, n_heads=n_heads,
                             tok_w=tok_w, seq_group=seq_group, n_alias=len(aliases), seq_minor=seq_minor)
    return pl.pallas_call(
        body,
        grid=(batch // bb, nch),
        in_specs=in_specs,
        out_specs=[
            row_spec(tok_w),
            conv_out_spec,
            s_spec,
        ],
        out_shape=[
            jax.ShapeDtypeStruct(y_shape, BF16),
            jax.ShapeDtypeStruct(conv_out_shape, F32),
            jax.ShapeDtypeStruct((n_layers, batch, n_heads, HEAD_DIM, HEAD_DIM), F32),
        ],
        scratch_shapes=[pltpu.VMEM((bb, SUBLANE + c_pad, main_w), F32), pltpu.VMEM((bb, c_pad, LANE), F32),
                        pltpu.VMEM((bb * c_real, tok_w), F32)],
        input_output_aliases=aliases,
        compiler_params=_params("arbitrary", "arbitrary"),
        name="delta_mixer",
    )(*args)


def _pool_body(*refs, c_real, c_pad, bb, n_past, tok_w, n_alias, seq_minor):
    u_ref, buf_ref, wg_ref, scale_ref = refs[:4]
    y_ref, pnew_ref, hist_scr, y_scr = refs[4 + n_alias:]
    c = c_pad
    padded = c_pad > c_real
    gw = tok_w // len(POOL_WINDOWS)

    @pl.when(pl.program_id(1) == 0)
    def _init():
        for s in range(bb):
            hist_scr[s, 0:1, :] = jnp.zeros((1, tok_w), F32)
            if seq_minor:
                for j in range(POOL_HIST - 1):
                    hist_scr[s, 1 + j:2 + j, :] = buf_ref[j, s:s + 1, :]
            else:
                hist_scr[s, 1:POOL_HIST, :] = buf_ref[s]
            if padded:
                hist_scr[s, POOL_HIST + c_real:POOL_HIST + c, :] = jnp.zeros((c - c_real, tok_w), F32)

    for s in range(bb):
        hist_scr[s, POOL_HIST:POOL_HIST + c_real, :] = u_ref[s * c_real:(s + 1) * c_real, :]
    pos = pl.program_id(1) * c_real + lax.broadcasted_iota(jnp.int32, (c, 1), 0)
    for gi, win in enumerate(POOL_WINDOWS):
        lo = gi * gw
        inv_cnt = 1.0 / jnp.minimum(win, pos + 1 + n_past).astype(F32)
        ds = []
        for s in range(bb):
            tok = hist_scr[s, POOL_HIST:POOL_HIST + c, lo:lo + gw]
            acc = tok
            for back in range(1, win):
                acc = acc + hist_scr[s, POOL_HIST - back:POOL_HIST - back + c, lo:lo + gw]
            ds.append(acc * inv_cnt - tok)
        d_all = ds[0] if bb == 1 else jnp.concatenate(ds, axis=0)
        y_all = _bdot(d_all, wg_ref[gi]) * scale_ref[:, lo:lo + gw]
        if padded:
            for s in range(bb):
                y_scr[s * c_real:(s + 1) * c_real, lo:lo + gw] = y_all[s * c:s * c + c_real]
        else:
            y_ref[:, lo:lo + gw] = y_all.astype(y_ref.dtype)

    if padded:
        y_ref[...] = y_scr[...].astype(y_ref.dtype)
    for s in range(bb):
        tail = hist_scr[s, c_real + 1:c_real + POOL_HIST, :]
        hist_scr[s, 1:POOL_HIST, :] = tail
        if seq_minor:
            for j in range(POOL_HIST - 1):
                pnew_ref[j, s:s + 1, :] = tail[j:j + 1, :]
        else:
            pnew_ref[s] = tail


def pool_mixer(u, row_off, buf, buf_layer, buf_out_prev, w_grp, scale, layer, y_prev, y_shape, *,
               batch, seq, chunk, bb, n_past, seq_minor):
    tok_w = buf.shape[-1]
    gw = tok_w // len(POOL_WINDOWS)
    c_real = min(chunk, seq)
    c_pad = max(c_real, SUBLANE)
    assert bb == 1 or c_real == seq
    nch = seq // c_real
    rows = bb * c_real
    blk0 = row_off // rows
    row_spec = pl.BlockSpec((rows, tok_w), lambda b, ch: (blk0 + b * nch + ch, 0))
    if seq_minor:
        buf_in_spec = pl.BlockSpec((None, POOL_HIST - 1, bb, tok_w), lambda b, ch: (buf_layer, 0, b, 0))
        buf_out_spec = buf_in_spec
        buf_out_shape = (buf.shape[0], POOL_HIST - 1, batch, tok_w)
    else:
        buf_in_spec = pl.BlockSpec((None, bb, POOL_HIST - 1, tok_w), lambda b, ch: (buf_layer, b, 0, 0))
        buf_out_spec = pl.BlockSpec((bb, POOL_HIST - 1, tok_w), lambda b, ch: (b, 0, 0))
        buf_out_shape = (batch, POOL_HIST - 1, tok_w)
    in_specs = [
        row_spec,
        buf_in_spec,
        pl.BlockSpec((None, len(POOL_WINDOWS), gw, gw), lambda b, ch: (layer, 0, 0, 0)),
        pl.BlockSpec((None, 1, tok_w), lambda b, ch: (layer, 0, 0)),
    ]
    args = [u, buf, w_grp, scale.reshape(scale.shape[0], 1, tok_w)]
    aliases = _alias_args(in_specs, args, y_prev, 0)
    aliases.update(_alias_args(in_specs, args, buf_out_prev, 1))
    body = functools.partial(_pool_body, c_real=c_real, c_pad=c_pad, bb=bb, n_past=n_past, tok_w=tok_w,
                             n_alias=len(aliases), seq_minor=seq_minor)
    return pl.pallas_call(
        body,
        grid=(batch // bb, nch),
        in_specs=in_specs,
        out_specs=[row_spec, buf_out_spec],
        out_shape=[jax.ShapeDtypeStruct(y_shape, BF16), jax.ShapeDtypeStruct(buf_out_shape, F32)],
        scratch_shapes=[pltpu.VMEM((bb, POOL_HIST + c_pad, tok_w), F32), pltpu.VMEM((rows, tok_w), F32)],
        input_output_aliases=aliases,
        compiler_params=_params("arbitrary", "arbitrary"),
        name="pool_mixer",
    )(*args)


def _softmax_rows(s):
    p = jnp.exp(s - jnp.max(s, axis=-1, keepdims=True))
    return p / jnp.sum(p, axis=-1, keepdims=True)


def _xattn_prompt_body(q_ref, k_ref, v_ref, y_prev_ref, o_ref):
    del y_prev_ref
    for h in range(N_MEM_HEADS):
        lo = h * HEAD_DIM
        s = _bdot_nt(q_ref[:, lo:lo + HEAD_DIM], k_ref[:, lo:lo + HEAD_DIM]) * (HEAD_DIM ** -0.5)
        o = _bdot(_softmax_rows(s), v_ref[:, lo:lo + HEAD_DIM])
        o_ref[:, lo:lo + HEAD_DIM] = o.astype(o_ref.dtype)


def cross_attn_prompt(q, q_col_block, kv, y_prev, *, batch, seq, tq):
    n_mem = kv.shape[1]
    tq = min(tq, seq)
    nq = seq // tq
    y_col_block = y_prev.shape[1] // MEM_W - 1
    return pl.pallas_call(
        _xattn_prompt_body,
        grid=(batch, nq),
        in_specs=[
            pl.BlockSpec((tq, MEM_W), lambda b, i: (b * nq + i, q_col_block)),
            pl.BlockSpec((None, n_mem, MEM_W), lambda b, i: (b, 0, 0)),
            pl.BlockSpec((None, n_mem, MEM_W), lambda b, i: (b, 0, 1)),
            pl.BlockSpec(memory_space=pl.ANY),
        ],
        out_specs=pl.BlockSpec((tq, MEM_W), lambda b, i: (b * nq + i, y_col_block)),
        out_shape=jax.ShapeDtypeStruct(y_prev.shape, y_prev.dtype),
        input_output_aliases={3: 0},
        compiler_params=_params("arbitrary", "arbitrary"),
        name="cross_attn_prompt",
    )(q, kv, kv, y_prev)


def _xattn_sample_body(q_ref, k_ref, v_ref, y_prev_ref, o_ref, q_scr, o_scr, *, bb, seq):
    del y_prev_ref
    n_rows = k_ref.shape[1]
    q_scr[...] = jnp.zeros(q_scr.shape, F32)
    for b in range(bb):
        q_scr[b, 0:seq, :] = q_ref[b * seq:(b + 1) * seq, :]
    shape = (N_MEM_HEADS * SUBLANE, n_rows)
    same_head = (lax.broadcasted_iota(jnp.int32, shape, 1) % N_MEM_HEADS
                 == lax.broadcasted_iota(jnp.int32, shape, 0) // SUBLANE)
    scores = []
    for b in range(bb):
        qb = q_scr[b]
        q_rows = jnp.concatenate([qb[:, h * HEAD_DIM:(h + 1) * HEAD_DIM] for h in range(N_MEM_HEADS)], axis=0)
        scores.append(_bdot_nt(q_rows, k_ref[b]) * (HEAD_DIM ** -0.5))
    probs = []
    for s in scores:
        p = jnp.exp(s - jnp.max(jnp.where(same_head, s, -jnp.inf), axis=-1, keepdims=True))
        p = jnp.where(same_head, p, 0.0)
        probs.append(p / jnp.sum(p, axis=-1, keepdims=True))
    outs = [_bdot(p, v_ref[b]) for b, p in enumerate(probs)]
    for b, o in enumerate(outs):
        for h in range(N_MEM_HEADS):
            o_scr[b * seq:(b + 1) * seq, h * HEAD_DIM:(h + 1) * HEAD_DIM] = o[h * SUBLANE:h * SUBLANE + seq]
    o_ref[...] = o_scr[...].astype(o_ref.dtype)


def cross_attn_sample(q, q_col_block, row_off, cache_k, cache_v, layer, y_prev, *, batch, seq, bb):
    n_rows = cache_k.shape[2]
    rows = bb * seq
    blk0 = row_off // rows
    y_col_block = y_prev.shape[1] // MEM_W - 1
    kv_spec = pl.BlockSpec((None, bb, n_rows, HEAD_DIM), lambda i: (layer, i, 0, 0))
    return pl.pallas_call(
        functools.partial(_xattn_sample_body, bb=bb, seq=seq),
        grid=(batch // bb,),
        in_specs=[pl.BlockSpec((rows, MEM_W), lambda i: (blk0 + i, q_col_block)), kv_spec, kv_spec,
                  pl.BlockSpec(memory_space=pl.ANY)],
        out_specs=pl.BlockSpec((rows, MEM_W), lambda i: (blk0 + i, y_col_block)),
        out_shape=jax.ShapeDtypeStruct(y_prev.shape, y_prev.dtype),
        scratch_shapes=[pltpu.VMEM((bb, SUBLANE, MEM_W), F32), pltpu.VMEM((rows, MEM_W), F32)],
        input_output_aliases={3: 0},
        compiler_params=_params("arbitrary"),
        name="cross_attn_sample",
    )(q, cache_k, cache_v, y_prev)


def kernel(x_prompt, x_sample, mem_prompt, state_delta_S, state_delta_conv, state_pool, cache_mem_k, cache_mem_v,
           norm_mix, norm_ffn, norm_mem, norm_final, w_in_delta, conv_w, a_log, dt_bias, delta_onorm,
           w_in_pool, w_pool_grp, pool_scale, w_mem_kv, w_out, w_gate_up, w_down):
    bp, lp, d = x_prompt.shape
    bs, ls, _ = x_sample.shape
    depth = w_out.shape[0]
    n_mem = mem_prompt.shape[1]
    n_delta, _, n_heads = state_delta_S.shape[:3]
    tok_w = n_heads * HEAD_DIM
    main_w = 4 * tok_w
    d_ff = w_down.shape[1]
    tp, ts = bp * lp, bs * ls
    n_past = min(PAST_LEN, POOL_HIST - 1)
    y_shape = (tp + ts, tok_w + MEM_W)

    x = (x_prompt.reshape(tp, d), x_sample.reshape(ts, d))

    mem_rows = mem_prompt.reshape(bp * n_mem, d)
    p_kv = [matmul(rmsnorm(mem_rows, norm_mem[l], BF16), w_mem_kv, l, ncols=2 * MEM_W, tn=MEM_W, out_dtype=F32)
            for l in range(depth)]
    p_mem_k = jnp.stack([kv[:, :MEM_W].reshape(bp, n_mem, N_MEM_HEADS, HEAD_DIM) for kv in p_kv])
    p_mem_v = jnp.stack([kv[:, MEM_W:].reshape(bp, n_mem, N_MEM_HEADS, HEAD_DIM) for kv in p_kv])
    cache_k = cache_mem_k.reshape(depth, bs, n_mem * N_MEM_HEADS, HEAD_DIM)
    cache_v = cache_mem_v.reshape(depth, bs, n_mem * N_MEM_HEADS, HEAD_DIM)

    w_in_delta_t = jnp.swapaxes(w_in_delta, 1, 2)
    zero_s = jnp.zeros((n_delta, bp, n_heads, HEAD_DIM, HEAD_DIM), F32)
    zero_conv = jnp.zeros((1, bp, CONV_W - 1, 3 * tok_w), F32)
    zero_pool = jnp.zeros((1, bp, POOL_HIST - 1, tok_w), F32)
    conv_in = jnp.swapaxes(state_delta_conv, 1, 2)
    pool_in = jnp.swapaxes(state_pool, 1, 2)

    p_conv, p_pool = [], []
    p_s = s_s = s_conv = s_pool = None
    di = pi = 0
    for l in range(depth):
        h = rmsnorm_concat(*x, norm_mix[l], BF16) if l == 0 else rmsnorm(x, norm_mix[l], BF16)
        if l % 2 == 0:
            proj = matmul(h, w_in_delta_t, di, ncols=main_w, tn=1024, out_dtype=F32, w_transposed=True)
            tail0 = main_w // LANE
            small = matmul(h, w_in_delta_t, di, ncols=MEM_W + LANE, tn=LANE, out_dtype=F32, w_transposed=True,
                           col_blocks=tuple(range(tail0, tail0 + (MEM_W + LANE) // LANE)),
                           out_slices=((2 * n_heads, 2 * n_heads + MEM_W), (0, LANE)))
            gates = (conv_w[di], a_log[di], dt_bias[di], delta_onorm[di])
            y, conv_p, p_s = delta_mixer(proj, small, 0, zero_conv, 0, None, zero_s, di, p_s, None, y_shape, *gates,
                                         batch=bp, seq=lp, chunk=64, bb=1, n_sub=4, seq_group=1, seq_minor=False)
            y, s_conv, s_s = delta_mixer(proj, small, tp, conv_in, di, s_conv, state_delta_S, di, s_s, y, y_shape,
                                         *gates, batch=bs, seq=ls, chunk=64, bb=8, n_sub=1, seq_group=4,
                                         seq_minor=True)
            p_conv.append(conv_p)
            q_src, q_blk = small, 0
            di += 1
        else:
            proj = matmul(h, w_in_pool, pi, ncols=tok_w + MEM_W, tn=1024, out_dtype=F32)
            y, pool_p = pool_mixer(proj, 0, zero_pool, 0, None, w_pool_grp, pool_scale, pi, None, y_shape,
                                   batch=bp, seq=lp, chunk=256, bb=1, n_past=0, seq_minor=False)
            y, s_pool = pool_mixer(proj, tp, pool_in, pi, s_pool, w_pool_grp, pool_scale, pi, y, y_shape,
                                   batch=bs, seq=ls, chunk=128, bb=16, n_past=n_past, seq_minor=True)
            p_pool.append(pool_p)
            q_src, q_blk = proj, tok_w // MEM_W
            pi += 1

        y = cross_attn_prompt(q_src, q_blk, p_kv[l].reshape(bp, n_mem, 2 * MEM_W), y, batch=bp, seq=lp, tq=1024)
        y = cross_attn_sample(q_src, q_blk, tp, cache_k, cache_v, l, y, batch=bs, seq=ls, bb=16)

        row_tiles = (256, 128) if l == 0 else (544, 512, 256, 128)
        fits = (lambda c: tp % c == 0 and ts % c == 0) if l == 0 else (lambda c: (tp + ts) % c == 0)
        x, h2 = matmul_residual_norm(y, w_out, l, x, norm_ffn[l], tm=next(c for c in row_tiles if fits(c)))
        act = matmul(h2, w_gate_up, l, ncols=d_ff, tn=512, out_dtype=BF16, col_blocks=(0, d_ff // 512), swiglu=True,
                     tm_max=2176)
        x = matmul(act, w_down, l, ncols=d, tn=512, out_dtype=F32, residual=x, tm_max=512)

    y_p, y_s = rmsnorm_split(x, norm_final, tp)
    return (y_p.reshape(bp, lp, d), y_s.reshape(bs, ls, d),
            p_s, jnp.stack(p_conv), jnp.stack(p_pool), p_mem_k, p_mem_v,
            s_s, jnp.swapaxes(s_conv, 1, 2), jnp.swapaxes(s_pool, 1, 2))
```

```python
import functools
import math

import jax
import jax.numpy as jnp
from jax import lax
from jax.experimental import pallas as pl
from jax.experimental.pallas import tpu as pltpu

F32 = jnp.float32
BF16 = jnp.bfloat16
HIGHEST = lax.Precision.HIGHEST

EPS = 1e-6
HEAD_DIM = 128
N_MEM_HEADS = 4
MEM_W = N_MEM_HEADS * HEAD_DIM
CONV_W = 4
POOL_WINDOWS = (2, 4, 8, 16)
POOL_HIST = max(POOL_WINDOWS)
PAST_LEN = 16384
LANE = 128
SUBLANE = 8
MXU_COLS = 256
CARRY_OFF = SUBLANE - (CONV_W - 1)
VMEM_LIMIT = 56 * 1024 * 1024


def _params(*sem):
    return pltpu.CompilerParams(dimension_semantics=sem, vmem_limit_bytes=VMEM_LIMIT)


def _sigmoid(x):
    return 1.0 / (1.0 + jnp.exp(-x))


def _silu(x):
    half = 0.5 * x
    return half + half * jnp.tanh(half)


def _dot(a, b, precision=None):
    return jnp.dot(a, b, preferred_element_type=F32, precision=precision)


def _dot_nt(a, b, precision=None):
    return lax.dot_general(a, b, (((1,), (1,)), ((), ())), preferred_element_type=F32, precision=precision)


def _dot_tn(a, b, precision=None):
    return lax.dot_general(a, b, (((0,), (0,)), ((), ())), preferred_element_type=F32, precision=precision)


def _bdot(a, b):
    return _dot(a.astype(BF16), b.astype(BF16))


def _bdot_nt(a, b):
    return _dot_nt(a.astype(BF16), b.astype(BF16))


def _bdot_tn(a, b):
    return _dot_tn(a.astype(BF16), b.astype(BF16))


def _alias_args(in_specs, args, prev, out_index):
    if prev is None:
        return {}
    in_specs.append(pl.BlockSpec(memory_space=pl.ANY))
    args.append(prev)
    return {len(args) - 1: out_index}


def _rmsnorm_body(x_ref, w_ref, o_ref):
    x = x_ref[...]
    ms = jnp.mean(x * x, axis=-1, keepdims=True)
    o_ref[...] = (x * lax.rsqrt(ms + EPS) * w_ref[...]).astype(o_ref.dtype)


def rmsnorm(x, w, out_dtype):
    t, d = x.shape
    tm = next(c for c in (1088, 1024, 512, 256, 128) if t % c == 0)
    return pl.pallas_call(
        _rmsnorm_body,
        grid=(t // tm,),
        in_specs=[pl.BlockSpec((tm, d), lambda i: (i, 0)), pl.BlockSpec((1, d), lambda i: (0, 0))],
        out_specs=pl.BlockSpec((tm, d), lambda i: (i, 0)),
        out_shape=jax.ShapeDtypeStruct((t, d), out_dtype),
        compiler_params=_params("arbitrary"),
        name="rmsnorm",
    )(x, w.reshape(1, d))


def _rmsnorm_concat_body(a_ref, b_ref, w_ref, o_ref, *, n_a):
    x = jnp.where(pl.program_id(0) < n_a, a_ref[...], b_ref[...])
    ms = jnp.mean(x * x, axis=-1, keepdims=True)
    o_ref[...] = (x * lax.rsqrt(ms + EPS) * w_ref[...]).astype(o_ref.dtype)


def rmsnorm_concat(xa, xb, w, out_dtype):
    (ra, d), rb = xa.shape, xb.shape[0]
    tm = next(c for c in (512, 256, 128) if ra % c == 0 and rb % c == 0)
    n_a = ra // tm
    return pl.pallas_call(
        functools.partial(_rmsnorm_concat_body, n_a=n_a),
        grid=((ra + rb) // tm,),
        in_specs=[pl.BlockSpec((tm, d), lambda i: (jnp.minimum(i, n_a - 1), 0)),
                  pl.BlockSpec((tm, d), lambda i: (jnp.maximum(i - n_a, 0), 0)),
                  pl.BlockSpec((1, d), lambda i: (0, 0))],
        out_specs=pl.BlockSpec((tm, d), lambda i: (i, 0)),
        out_shape=jax.ShapeDtypeStruct((ra + rb, d), out_dtype),
        compiler_params=_params("arbitrary"),
        name="rmsnorm_concat",
    )(xa, xb, w.reshape(1, d))


def _rmsnorm_split_body(x_ref, w_ref, a_ref, b_ref, *, n_a):
    x = x_ref[...]
    ms = jnp.mean(x * x, axis=-1, keepdims=True)
    y = x * lax.rsqrt(ms + EPS) * w_ref[...]

    @pl.when(pl.program_id(0) < n_a)
    def _first():
        a_ref[...] = y

    @pl.when(pl.program_id(0) >= n_a)
    def _second():
        b_ref[...] = y


def rmsnorm_split(x, w, rows_a):
    t, d = x.shape
    tm = next(c for c in (512, 256, 128) if rows_a % c == 0 and (t - rows_a) % c == 0)
    n_a = rows_a // tm
    return pl.pallas_call(
        functools.partial(_rmsnorm_split_body, n_a=n_a),
        grid=(t // tm,),
        in_specs=[pl.BlockSpec((tm, d), lambda i: (i, 0)), pl.BlockSpec((1, d), lambda i: (0, 0))],
        out_specs=[pl.BlockSpec((tm, d), lambda i: (jnp.minimum(i, n_a - 1), 0)),
                   pl.BlockSpec((tm, d), lambda i: (jnp.maximum(i - n_a, 0), 0))],
        out_shape=[jax.ShapeDtypeStruct((rows_a, d), F32), jax.ShapeDtypeStruct((t - rows_a, d), F32)],
        compiler_params=_params("arbitrary"),
        name="rmsnorm_split",
    )(x, w.reshape(1, d))


def _mm_body(*refs, n_w, tn, has_res, swiglu, out_slices, slab_valid, w_transposed):
    x_ref = refs[0]
    w_refs = refs[1:1 + n_w]
    res_ref = refs[1 + n_w] if has_res else None
    o_ref = refs[-2]
    wb_ref = refs[-1]

    @pl.when(pl.program_id(1) == 0)
    def _cast_weights():
        for i, w_ref in enumerate(w_refs):
            valid = slab_valid[i]
            if w_transposed:
                wb_ref[i * tn:i * tn + valid, :] = w_ref[0:valid, :].astype(BF16)
                if valid < tn:
                    wb_ref[i * tn + valid:(i + 1) * tn, :] = jnp.zeros((tn - valid, wb_ref.shape[1]), BF16)
            else:
                wb_ref[:, i * tn:i * tn + valid] = w_ref[:, 0:valid].astype(BF16)
                if valid < tn:
                    wb_ref[:, i * tn + valid:(i + 1) * tn] = jnp.zeros((wb_ref.shape[0], tn - valid), BF16)

    def product(lo, hi):
        if w_transposed:
            return _dot_nt(x_ref[...], wb_ref[lo:hi, :])
        return _dot(x_ref[...], wb_ref[:, lo:hi])

    if out_slices is not None:
        acc = product(0, n_w * tn)
        acc = jnp.concatenate([acc[:, lo:hi] for lo, hi in out_slices], axis=1)
        o_ref[...] = acc.astype(o_ref.dtype)
        return
    for c0 in range(0, tn, MXU_COLS):
        c1 = min(c0 + MXU_COLS, tn)
        acc = product(c0, c1)
        if swiglu:
            acc = _silu(acc) * product(tn + c0, tn + c1)
        if has_res:
            acc = acc + res_ref[:, c0:c1]
        o_ref[:, c0:c1] = acc.astype(o_ref.dtype)


def matmul(x, w, layer, *, ncols, tn, out_dtype, col_blocks=(0,), residual=None, swiglu=False, out_slices=None,
           w_transposed=False, tm_max=1088):
    t, k = x.shape
    tm = next(c for c in (2176, 1088, 1024, 512, 256, 128) if c <= tm_max and t % c == 0)
    n_w = len(col_blocks)
    to = tn if out_slices is None else sum(hi - lo for lo, hi in out_slices)
    in_specs = [pl.BlockSpec((tm, k), lambda j, m: (m, 0))]
    for cb in col_blocks:
        if w_transposed:
            in_specs.append(pl.BlockSpec((None, tn, k), lambda j, m, cb=cb: (layer, cb + j, 0)))
        else:
            in_specs.append(pl.BlockSpec((None, k, tn), lambda j, m, cb=cb: (layer, 0, cb + j)))
    args = [x] + [w] * n_w
    if residual is not None:
        in_specs.append(pl.BlockSpec((tm, to), lambda j, m: (m, j)))
        args.append(residual)
    n_tiles = ncols // to
    n_total = w.shape[1] if w_transposed else w.shape[2]
    slab_valid = tuple(min(tn, n_total - (cb + n_tiles - 1) * tn) for cb in col_blocks)
    assert all(v == tn for v in slab_valid) or n_tiles == 1
    body = functools.partial(_mm_body, n_w=n_w, tn=tn, has_res=residual is not None, swiglu=swiglu,
                             out_slices=out_slices, slab_valid=slab_valid, w_transposed=w_transposed)
    return pl.pallas_call(
        body,
        grid=(n_tiles, t // tm),
        in_specs=in_specs,
        out_specs=pl.BlockSpec((tm, to), lambda j, m: (m, j)),
        out_shape=jax.ShapeDtypeStruct((t, ncols), out_dtype),
        scratch_shapes=[pltpu.VMEM((n_w * tn, k) if w_transposed else (k, n_w * tn), BF16)],
        compiler_params=_params("arbitrary", "arbitrary"),
        name="matmul",
    )(*args)


def _proj_norm_body(*refs, n_a):
    x_ref, w_ref, res_ref = refs[:3]
    res_b_ref = refs[3] if n_a is not None else None
    nw_ref, o_ref, h_ref, wb_ref = refs[-4:]

    def residual(c0):
        res = res_ref[:, c0:c0 + MXU_COLS]
        if res_b_ref is None:
            return res
        return jnp.where(pl.program_id(0) < n_a, res, res_b_ref[:, c0:c0 + MXU_COLS])

    @pl.when(pl.program_id(0) == 0)
    def _cast_weights():
        wb_ref[...] = w_ref[...].astype(BF16)

    n = o_ref.shape[1]
    sumsq = jnp.zeros((o_ref.shape[0], LANE), F32)
    for c0 in range(0, n, MXU_COLS):
        acc = _dot(x_ref[...], wb_ref[:, c0:c0 + MXU_COLS]) + residual(c0)
        o_ref[:, c0:c0 + MXU_COLS] = acc
        sq = acc * acc
        for l0 in range(0, MXU_COLS, LANE):
            sumsq = sumsq + sq[:, l0:l0 + LANE]
    scale = lax.rsqrt(jnp.sum(sumsq, axis=-1, keepdims=True) / n + EPS)
    for c0 in range(0, n, MXU_COLS):
        h = o_ref[:, c0:c0 + MXU_COLS] * scale * nw_ref[:, c0:c0 + MXU_COLS]
        h_ref[:, c0:c0 + MXU_COLS] = h.astype(h_ref.dtype)


def matmul_residual_norm(x, w, layer, residual, norm_w, *, tm):
    t, k = x.shape
    n = w.shape[2]
    in_specs = [
        pl.BlockSpec((tm, k), lambda m: (m, 0)),
        pl.BlockSpec((None, k, n), lambda m: (layer, 0, 0), pipeline_mode=pl.Buffered(1)),
    ]
    if isinstance(residual, tuple):
        n_a = residual[0].shape[0] // tm
        in_specs += [pl.BlockSpec((tm, n), lambda m: (jnp.minimum(m, n_a - 1), 0)),
                     pl.BlockSpec((tm, n), lambda m: (jnp.maximum(m - n_a, 0), 0))]
    else:
        n_a, residual = None, (residual,)
        in_specs.append(pl.BlockSpec((tm, n), lambda m: (m, 0)))
    in_specs.append(pl.BlockSpec((1, n), lambda m: (0, 0)))
    return pl.pallas_call(
        functools.partial(_proj_norm_body, n_a=n_a),
        grid=(t // tm,),
        in_specs=in_specs,
        out_specs=[pl.BlockSpec((tm, n), lambda m: (m, 0)), pl.BlockSpec((tm, n), lambda m: (m, 0))],
        out_shape=[jax.ShapeDtypeStruct((t, n), F32), jax.ShapeDtypeStruct((t, n), BF16)],
        scratch_shapes=[pltpu.VMEM((k, n), BF16)],
        compiler_params=_params("arbitrary"),
        name="matmul_residual_norm",
    )(x, w, *residual, norm_w.reshape(1, n))


def _delta_body(*refs, c_real, c_pad, bb, n_sub, n_heads, tok_w, seq_group, n_alias, seq_minor):
    main_ref, small_ref, cbuf_ref, s0_ref, cw_ref, alog_ref, dtb_ref, onw_ref = refs[:8]
    y_ref, cnew_ref, s_ref, xp_scr, ab_scr, y_scr = refs[8 + n_alias:]
    qkv_w = 3 * tok_w
    c = c_pad
    padded = c_pad > c_real
    assert not (padded and n_sub > 1)
    seq_rows = n_sub * c_real

    @pl.when(pl.program_id(1) == 0)
    def _init():
        s_ref[...] = s0_ref[...]
        for s in range(bb):
            xp_scr[s, 0:CARRY_OFF, :] = jnp.zeros((CARRY_OFF, xp_scr.shape[2]), F32)
            if seq_minor:
                for i in range(CONV_W - 1):
                    xp_scr[s, CARRY_OFF + i:CARRY_OFF + i + 1, 0:qkv_w] = cbuf_ref[i, s:s + 1, :]
            else:
                xp_scr[s, CARRY_OFF:SUBLANE, 0:qkv_w] = cbuf_ref[s]
            if padded:
                xp_scr[s, SUBLANE + c_real:SUBLANE + c, :] = jnp.zeros((c - c_real, xp_scr.shape[2]), F32)
        if padded:
            ab_scr[...] = jnp.zeros(ab_scr.shape, F32)

    if padded:
        for s in range(bb):
            xp_scr[s, SUBLANE:SUBLANE + c_real, :] = main_ref[s * c_real:(s + 1) * c_real, :]
            ab_scr[s, 0:c_real, :] = small_ref[s * c_real:(s + 1) * c_real, MEM_W:MEM_W + LANE]

    def chunk_cols(s, j, lo, hi):
        if padded:
            return xp_scr[s, SUBLANE:SUBLANE + c, lo:hi]
        r0 = s * seq_rows + j * c
        return main_ref[r0:r0 + c, lo:hi]

    def conv_rows(s, j, col):
        if padded or j == 0:
            return jnp.concatenate([xp_scr[s, 0:SUBLANE, col:col + LANE], chunk_cols(s, 0, col, col + LANE)], axis=0)
        r0 = s * seq_rows + j * c
        return main_ref[r0 - SUBLANE:r0 + c, col:col + LANE]

    def conv_slab(s, j, col):
        blk = conv_rows(s, j, col)
        n = SUBLANE + c
        acc = blk[SUBLANE:] * cw_ref[CONV_W - 1:CONV_W, col:col + LANE]
        for i in range(CONV_W - 1):
            tap = pltpu.roll(blk, n - (CARRY_OFF + i), axis=0)[:c]
            acc = acc + tap * cw_ref[i:i + 1, col:col + LANE]
        return _silu(acc)

    rows = lax.broadcasted_iota(jnp.int32, (c, c), 0)
    cols = lax.broadcasted_iota(jnp.int32, (c, c), 1)
    incl = rows >= cols
    strict = rows > cols
    eye_l = (lax.broadcasted_iota(jnp.int32, (LANE, LANE), 0)
             == lax.broadcasted_iota(jnp.int32, (LANE, LANE), 1)).astype(F32)
    n_levels = int(math.log2(c))

    for s0 in range(0, bb, seq_group):
        seqs = range(s0, min(s0 + seq_group, bb))
        chunks = [(s, j) for s in seqs for j in range(n_sub)]
        units = [(s, j, h) for (s, j) in chunks for h in range(n_heads)]

        beta_all, gc_all, gc_t, gam_all, kdec_all, gend_all = {}, {}, {}, {}, {}, {}
        for sj in chunks:
            s, j = sj
            if padded:
                ab = ab_scr[s]
            else:
                r0 = s * seq_rows + j * c
                ab = small_ref[r0:r0 + c, MEM_W:MEM_W + LANE]
            a_in = ab + dtb_ref[...]
            softplus = jnp.maximum(a_in, 0.0) + jnp.log(1.0 + jnp.exp(-jnp.abs(a_in)))
            g = -jnp.exp(alog_ref[...]) * softplus
            beta = _sigmoid(ab)
            if padded:
                live = lax.broadcasted_iota(jnp.int32, (c, LANE), 0) < c_real
                g = jnp.where(live, g, 0.0)
                beta = jnp.where(live, beta, 0.0)
            gc = _dot(incl.astype(F32), g, HIGHEST)
            g_last = gc[c - 1:c, :]
            beta_all[sj] = beta
            gc_all[sj] = gc
            gc_t[sj] = _dot_nt(eye_l, gc, HIGHEST)
            gam_all[sj] = jnp.exp(gc)
            kdec_all[sj] = jnp.exp(g_last - gc)
            gend_all[sj] = jnp.exp(g_last)

        q, k, v, beta, gam, decay = {}, {}, {}, {}, {}, {}
        for un in units:
            s, j, h = un
            lo = h * HEAD_DIM
            qh = conv_slab(s, j, lo)
            kh = conv_slab(s, j, tok_w + lo)
            v[un] = conv_slab(s, j, 2 * tok_w + lo)
            q[un] = qh * lax.rsqrt(jnp.sum(qh * qh, axis=-1, keepdims=True) + EPS) * (HEAD_DIM ** -0.5)
            k[un] = kh * lax.rsqrt(jnp.sum(kh * kh, axis=-1, keepdims=True) + EPS)
            beta[un] = beta_all[s, j][:, n_heads + h:n_heads + h + 1]
            gam[un] = gam_all[s, j][:, h:h + 1]
            diff = gc_all[s, j][:, h:h + 1] - gc_t[s, j][h:h + 1, :]
            decay[un] = jnp.where(incl, jnp.exp(jnp.where(incl, diff, 0.0)), 0.0)

        k_b = {un: k[un].astype(BF16) for un in units}
        qk_kk = {un: _dot_nt(jnp.concatenate([q[un].astype(BF16), k_b[un]], axis=0), k_b[un]) for un in units}

        x_pow = {un: -(jnp.where(strict, qk_kk[un][c:] * decay[un], 0.0) * beta[un]) for un in units}
        t_off = dict(x_pow)
        for lvl in range(1, n_levels + 1):
            for un in units:
                x_b = x_pow[un].astype(BF16)
                if lvl == 1:
                    x_pow[un] = _dot(x_b, x_b)
                elif lvl < n_levels:
                    both = _dot(jnp.concatenate([t_off[un].astype(BF16), x_b], axis=0), x_b)
                    t_off[un] = t_off[un] + x_pow[un] + both[:c]
                    x_pow[un] = both[c:]
                else:
                    t_off[un] = t_off[un] + x_pow[un] + _dot(t_off[un].astype(BF16), x_b)

        sol = {}
        for un in units:
            rhs = jnp.concatenate([v[un] * beta[un], k[un] * (beta[un] * gam[un])], axis=1)
            sol[un] = rhs + _bdot(t_off[un], rhs)

        state = {(s, h): s_ref[s, h] for s in seqs for h in range(n_heads)}
        o = {}
        for j in range(n_sub):
            now = [(s, j, h) for s in seqs for h in range(n_heads)]
            from_state = {un: _bdot(jnp.concatenate([sol[un][:, HEAD_DIM:], q[un] * gam[un]], axis=0),
                                    state[un[0], un[2]]) for un in now}
            u = {un: sol[un][:, :HEAD_DIM] - from_state[un][:c] for un in now}
            for un in now:
                o[un] = from_state[un][c:] + _bdot(jnp.where(incl, qk_kk[un][:c] * decay[un], 0.0), u[un])
            for un in now:
                s, _, h = un
                state[s, h] = (state[s, h] * gend_all[s, j][:, h:h + 1]
                               + _bdot_tn(k[un] * kdec_all[s, j][:, h:h + 1], u[un]))
        for (s, h), val in state.items():
            s_ref[s, h] = val

        for un in units:
            s, j, h = un
            lo = h * HEAD_DIM
            oh = o[un] * lax.rsqrt(jnp.mean(o[un] * o[un], axis=-1, keepdims=True) + EPS) * onw_ref[...]
            out = oh * _silu(chunk_cols(s, j, qkv_w + lo, qkv_w + lo + HEAD_DIM))
            if padded:
                y_scr[s * c_real:(s + 1) * c_real, lo:lo + HEAD_DIM] = out[:c_real]
            else:
                r0 = s * seq_rows + j * c
                y_ref[r0:r0 + c, lo:lo + HEAD_DIM] = out.astype(y_ref.dtype)

    if padded:
        y_ref[...] = y_scr[...].astype(y_ref.dtype)
    for s in range(bb):
        if padded:
            carry = xp_scr[s, CARRY_OFF + c_real:SUBLANE + c_real, 0:qkv_w]
        else:
            carry = main_ref[(s + 1) * seq_rows - (CONV_W - 1):(s + 1) * seq_rows, 0:qkv_w]
        xp_scr[s, CARRY_OFF:SUBLANE, 0:qkv_w] = carry
        if seq_minor:
            for i in range(CONV_W - 1):
                cnew_ref[i, s:s + 1, :] = carry[i:i + 1, :]
        else:
            cnew_ref[s] = carry


def delta_mixer(main, small, row_off, conv_buf, conv_layer, conv_out_prev, s0, s_layer, s_out_prev, y_prev, y_shape,
                conv_w, a_log, dt_bias, onorm_w, *, batch, seq, chunk, bb, n_sub, seq_group, seq_minor):
    n_layers, _, n_heads = s0.shape[:3]
    tok_w = n_heads * HEAD_DIM
    main_w = 4 * tok_w
    qkv_w = 3 * tok_w
    c_real = min(chunk, seq)
    c_pad = max(c_real, SUBLANE)
    assert bb == 1 or c_real == seq
    nch = seq // (n_sub * c_real)
    rows = bb * n_sub * c_real
    blk0 = row_off // rows
    row_spec = lambda w: pl.BlockSpec((rows, w), lambda b, ch: (blk0 + b * nch + ch, 0))
    pad_lane = lambda p: jnp.zeros((1, LANE), F32).at[0, :n_heads].set(p.astype(F32))
    s_spec = pl.BlockSpec((None, bb, n_heads, HEAD_DIM, HEAD_DIM), lambda b, ch: (s_layer, b, 0, 0, 0))
    if seq_minor:
        conv_in_spec = pl.BlockSpec((None, CONV_W - 1, bb, qkv_w), lambda b, ch: (conv_layer, 0, b, 0))
        conv_out_spec = conv_in_spec
        conv_out_shape = (conv_buf.shape[0], CONV_W - 1, batch, qkv_w)
    else:
        conv_in_spec = pl.BlockSpec((None, bb, CONV_W - 1, qkv_w), lambda b, ch: (conv_layer, b, 0, 0))
        conv_out_spec = pl.BlockSpec((bb, CONV_W - 1, qkv_w), lambda b, ch: (b, 0, 0))
        conv_out_shape = (batch, CONV_W - 1, qkv_w)
    in_specs = [
        row_spec(main_w),
        row_spec(small.shape[-1]),
        conv_in_spec,
        s_spec,
        pl.BlockSpec((CONV_W, qkv_w), lambda b, ch: (0, 0)),
        pl.BlockSpec((1, LANE), lambda b, ch: (0, 0)),
        pl.BlockSpec((1, LANE), lambda b, ch: (0, 0)),
        pl.BlockSpec((1, HEAD_DIM), lambda b, ch: (0, 0)),
    ]
    args = [main, small, conv_buf, s0, conv_w, pad_lane(a_log), pad_lane(dt_bias), onorm_w.reshape(1, HEAD_DIM)]
    aliases = {}
    aliases.update(_alias_args(in_specs, args, y_prev, 0))
    aliases.update(_alias_args(in_specs, args, conv_out_prev, 1))
    aliases.update(_alias_args(in_specs, args, s_out_prev, 2))
    body = functools.partial(_delta_body, c_real=c_real, c_pad=c_pad, bb=bb, n_sub=n_sub, n_heads=n_heads,
                             tok_w=tok_w, seq_group=seq_group, n_alias=len(aliases), seq_minor=seq_minor)
    return pl.pallas_call(
        body,
        grid=(batch // bb, nch),
        in_specs=in_specs,
        out_specs=[
            row_spec(tok_w),
            conv_out_spec,
            s_spec,
        ],
        out_shape=[
            jax.ShapeDtypeStruct(y_shape, BF16),
            jax.ShapeDtypeStruct(conv_out_shape, F32),
            jax.ShapeDtypeStruct((n_layers, batch, n_heads, HEAD_DIM, HEAD_DIM), F32),
        ],
        scratch_shapes=[pltpu.VMEM((bb, SUBLANE + c_pad, main_w), F32), pltpu.VMEM((bb, c_pad, LANE), F32),
                        pltpu.VMEM((bb * c_real, tok_w), F32)],
        input_output_aliases=aliases,
        compiler_params=_params("arbitrary", "arbitrary"),
        name="delta_mixer",
    )(*args)


def _pool_body(*refs, c_real, c_pad, bb, n_past, tok_w, n_alias, seq_minor):
    u_ref, buf_ref, wg_ref, scale_ref = refs[:4]
    y_ref, pnew_ref, hist_scr, y_scr = refs[4 + n_alias:]
    c = c_pad
    padded = c_pad > c_real
    gw = tok_w // len(POOL_WINDOWS)

    @pl.when(pl.program_id(1) == 0)
    def _init():
        for s in range(bb):
            hist_scr[s, 0:1, :] = jnp.zeros((1, tok_w), F32)
            if seq_minor:
                for j in range(POOL_HIST - 1):
                    hist_scr[s, 1 + j:2 + j, :] = buf_ref[j, s:s + 1, :]
            else:
                hist_scr[s, 1:POOL_HIST, :] = buf_ref[s]
            if padded:
                hist_scr[s, POOL_HIST + c_real:POOL_HIST + c, :] = jnp.zeros((c - c_real, tok_w), F32)

    for s in range(bb):
        hist_scr[s, POOL_HIST:POOL_HIST + c_real, :] = u_ref[s * c_real:(s + 1) * c_real, :]
    pos = pl.program_id(1) * c_real + lax.broadcasted_iota(jnp.int32, (c, 1), 0)
    for gi, win in enumerate(POOL_WINDOWS):
        lo = gi * gw
        inv_cnt = 1.0 / jnp.minimum(win, pos + 1 + n_past).astype(F32)
        ds = []
        for s in range(bb):
            tok = hist_scr[s, POOL_HIST:POOL_HIST + c, lo:lo + gw]
            acc = tok
            for back in range(1, win):
                acc = acc + hist_scr[s, POOL_HIST - back:POOL_HIST - back + c, lo:lo + gw]
            ds.append(acc * inv_cnt - tok)
        d_all = ds[0] if bb == 1 else jnp.concatenate(ds, axis=0)
        y_all = _bdot(d_all, wg_ref[gi]) * scale_ref[:, lo:lo + gw]
        if padded:
            for s in range(bb):
                y_scr[s * c_real:(s + 1) * c_real, lo:lo + gw] = y_all[s * c:s * c + c_real]
        else:
            y_ref[:, lo:lo + gw] = y_all.astype(y_ref.dtype)

    if padded:
        y_ref[...] = y_scr[...].astype(y_ref.dtype)
    for s in range(bb):
        tail = hist_scr[s, c_real + 1:c_real + POOL_HIST, :]
        hist_scr[s, 1:POOL_HIST, :] = tail
        if seq_minor:
            for j in range(POOL_HIST - 1):
                pnew_ref[j, s:s + 1, :] = tail[j:j + 1, :]
        else:
            pnew_ref[s] = tail


def pool_mixer(u, row_off, buf, buf_layer, buf_out_prev, w_grp, scale, layer, y_prev, y_shape, *,
               batch, seq, chunk, bb, n_past, seq_minor):
    tok_w = buf.shape[-1]
    gw = tok_w // len(POOL_WINDOWS)
    c_real = min(chunk, seq)
    c_pad = max(c_real, SUBLANE)
    assert bb == 1 or c_real == seq
    nch = seq // c_real
    rows = bb * c_real
    blk0 = row_off // rows
    row_spec = pl.BlockSpec((rows, tok_w), lambda b, ch: (blk0 + b * nch + ch, 0))
    if seq_minor:
        buf_in_spec = pl.BlockSpec((None, POOL_HIST - 1, bb, tok_w), lambda b, ch: (buf_layer, 0, b, 0))
        buf_out_spec = buf_in_spec
        buf_out_shape = (buf.shape[0], POOL_HIST - 1, batch, tok_w)
    else:
        buf_in_spec = pl.BlockSpec((None, bb, POOL_HIST - 1, tok_w), lambda b, ch: (buf_layer, b, 0, 0))
        buf_out_spec = pl.BlockSpec((bb, POOL_HIST - 1, tok_w), lambda b, ch: (b, 0, 0))
        buf_out_shape = (batch, POOL_HIST - 1, tok_w)
    in_specs = [
        row_spec,
        buf_in_spec,
        pl.BlockSpec((None, len(POOL_WINDOWS), gw, gw), lambda b, ch: (layer, 0, 0, 0)),
        pl.BlockSpec((None, 1, tok_w), lambda b, ch: (layer, 0, 0)),
    ]
    args = [u, buf, w_grp, scale.reshape(scale.shape[0], 1, tok_w)]
    aliases = _alias_args(in_specs, args, y_prev, 0)
    aliases.update(_alias_args(in_specs, args, buf_out_prev, 1))
    body = functools.partial(_pool_body, c_real=c_real, c_pad=c_pad, bb=bb, n_past=n_past, tok_w=tok_w,
                             n_alias=len(aliases), seq_minor=seq_minor)
    return pl.pallas_call(
        body,
        grid=(batch // bb, nch),
        in_specs=in_specs,
        out_specs=[row_spec, buf_out_spec],
        out_shape=[jax.ShapeDtypeStruct(y_shape, BF16), jax.ShapeDtypeStruct(buf_out_shape, F32)],
        scratch_shapes=[pltpu.VMEM((bb, POOL_HIST + c_pad, tok_w), F32), pltpu.VMEM((rows, tok_w), F32)],
        input_output_aliases=aliases,
        compiler_params=_params("arbitrary", "arbitrary"),
        name="pool_mixer",
    )(*args)


def _mem_kv_body(x_ref, nw_ref, w_ref, kv_ref, k_ref, v_ref):
    x = x_ref[...]
    n_rows = x.shape[0]
    ms = jnp.mean(x * x, axis=-1, keepdims=True)
    h = (x * lax.rsqrt(ms + EPS) * nw_ref[...]).astype(BF16)
    for c0 in range(0, 2 * MEM_W, MXU_COLS):
        acc = _dot(h, w_ref[:, c0:c0 + MXU_COLS].astype(BF16))
        kv_ref[:, c0:c0 + MXU_COLS] = acc
        for l0 in range(0, MXU_COLS, HEAD_DIM):
            col = c0 + l0
            head = (col % MEM_W) // HEAD_DIM
            dst = k_ref if col < MEM_W else v_ref
            dst[pl.ds(head, n_rows, stride=N_MEM_HEADS), :] = acc[:, l0:l0 + HEAD_DIM]


def mem_kv(mem_rows, norm_w, w_kv):
    depth, d, n = w_kv.shape
    rows = mem_rows.shape[0]
    head_rows = jax.ShapeDtypeStruct((depth, rows * N_MEM_HEADS, HEAD_DIM), F32)
    head_spec = pl.BlockSpec((None, rows * N_MEM_HEADS, HEAD_DIM), lambda l: (l, 0, 0))
    return pl.pallas_call(
        _mem_kv_body,
        grid=(depth,),
        in_specs=[pl.BlockSpec((rows, d), lambda l: (0, 0), pipeline_mode=pl.Buffered(1)),
                  pl.BlockSpec((None, 1, d), lambda l: (l, 0, 0)),
                  pl.BlockSpec((None, d, n), lambda l: (l, 0, 0))],
        out_specs=[pl.BlockSpec((None, rows, n), lambda l: (l, 0, 0)), head_spec, head_spec],
        out_shape=[jax.ShapeDtypeStruct((depth, rows, n), F32), head_rows, head_rows],
        compiler_params=_params("arbitrary"),
        name="mem_kv",
    )(mem_rows, norm_w.reshape(depth, 1, d), w_kv)


def _softmax_rows(s):
    p = jnp.exp(s - jnp.max(s, axis=-1, keepdims=True))
    return p / jnp.sum(p, axis=-1, keepdims=True)


def _xattn_prompt_body(q_ref, k_ref, v_ref, y_prev_ref, o_ref):
    del y_prev_ref
    for h in range(N_MEM_HEADS):
        lo = h * HEAD_DIM
        s = _bdot_nt(q_ref[:, lo:lo + HEAD_DIM], k_ref[:, lo:lo + HEAD_DIM]) * (HEAD_DIM ** -0.5)
        o = _bdot(_softmax_rows(s), v_ref[:, lo:lo + HEAD_DIM])
        o_ref[:, lo:lo + HEAD_DIM] = o.astype(o_ref.dtype)


def cross_attn_prompt(q, q_col_block, kv, layer, y_prev, *, batch, seq, tq):
    n_mem = kv.shape[2]
    tq = min(tq, seq)
    nq = seq // tq
    y_col_block = y_prev.shape[1] // MEM_W - 1
    return pl.pallas_call(
        _xattn_prompt_body,
        grid=(batch, nq),
        in_specs=[
            pl.BlockSpec((tq, MEM_W), lambda b, i: (b * nq + i, q_col_block)),
            pl.BlockSpec((None, None, n_mem, MEM_W), lambda b, i: (layer, b, 0, 0)),
            pl.BlockSpec((None, None, n_mem, MEM_W), lambda b, i: (layer, b, 0, 1)),
            pl.BlockSpec(memory_space=pl.ANY),
        ],
        out_specs=pl.BlockSpec((tq, MEM_W), lambda b, i: (b * nq + i, y_col_block)),
        out_shape=jax.ShapeDtypeStruct(y_prev.shape, y_prev.dtype),
        input_output_aliases={3: 0},
        compiler_params=_params("arbitrary", "arbitrary"),
        name="cross_attn_prompt",
    )(q, kv, kv, y_prev)


def _xattn_sample_body(q_ref, k_ref, v_ref, y_prev_ref, o_ref, q_scr, o_scr, *, bb, seq):
    del y_prev_ref
    n_rows = k_ref.shape[1]
    q_scr[...] = jnp.zeros(q_scr.shape, F32)
    for b in range(bb):
        q_scr[b, 0:seq, :] = q_ref[b * seq:(b + 1) * seq, :]
    shape = (N_MEM_HEADS * SUBLANE, n_rows)
    same_head = (lax.broadcasted_iota(jnp.int32, shape, 1) % N_MEM_HEADS
                 == lax.broadcasted_iota(jnp.int32, shape, 0) // SUBLANE)
    scores = []
    for b in range(bb):
        qb = q_scr[b]
        q_rows = jnp.concatenate([qb[:, h * HEAD_DIM:(h + 1) * HEAD_DIM] for h in range(N_MEM_HEADS)], axis=0)
        scores.append(_bdot_nt(q_rows, k_ref[b]) * (HEAD_DIM ** -0.5))
    probs = []
    for s in scores:
        p = jnp.exp(s - jnp.max(jnp.where(same_head, s, -jnp.inf), axis=-1, keepdims=True))
        p = jnp.where(same_head, p, 0.0)
        probs.append(p / jnp.sum(p, axis=-1, keepdims=True))
    outs = [_bdot(p, v_ref[b]) for b, p in enumerate(probs)]
    for b, o in enumerate(outs):
        for h in range(N_MEM_HEADS):
            o_scr[b * seq:(b + 1) * seq, h * HEAD_DIM:(h + 1) * HEAD_DIM] = o[h * SUBLANE:h * SUBLANE + seq]
    o_ref[...] = o_scr[...].astype(o_ref.dtype)


def cross_attn_sample(q, q_col_block, row_off, cache_k, cache_v, layer, y_prev, *, batch, seq, bb):
    n_rows = cache_k.shape[2]
    rows = bb * seq
    blk0 = row_off // rows
    y_col_block = y_prev.shape[1] // MEM_W - 1
    kv_spec = pl.BlockSpec((None, bb, n_rows, HEAD_DIM), lambda i: (layer, i, 0, 0))
    return pl.pallas_call(
        functools.partial(_xattn_sample_body, bb=bb, seq=seq),
        grid=(batch // bb,),
        in_specs=[pl.BlockSpec((rows, MEM_W), lambda i: (blk0 + i, q_col_block)), kv_spec, kv_spec,
                  pl.BlockSpec(memory_space=pl.ANY)],
        out_specs=pl.BlockSpec((rows, MEM_W), lambda i: (blk0 + i, y_col_block)),
        out_shape=jax.ShapeDtypeStruct(y_prev.shape, y_prev.dtype),
        scratch_shapes=[pltpu.VMEM((bb, SUBLANE, MEM_W), F32), pltpu.VMEM((rows, MEM_W), F32)],
        input_output_aliases={3: 0},
        compiler_params=_params("arbitrary"),
        name="cross_attn_sample",
    )(q, cache_k, cache_v, y_prev)


def kernel(x_prompt, x_sample, mem_prompt, state_delta_S, state_delta_conv, state_pool, cache_mem_k, cache_mem_v,
           norm_mix, norm_ffn, norm_mem, norm_final, w_in_delta, conv_w, a_log, dt_bias, delta_onorm,
           w_in_pool, w_pool_grp, pool_scale, w_mem_kv, w_out, w_gate_up, w_down):
    bp, lp, d = x_prompt.shape
    bs, ls, _ = x_sample.shape
    depth = w_out.shape[0]
    n_mem = mem_prompt.shape[1]
    n_delta, _, n_heads = state_delta_S.shape[:3]
    tok_w = n_heads * HEAD_DIM
    main_w = 4 * tok_w
    d_ff = w_down.shape[1]
    tp, ts = bp * lp, bs * ls
    n_past = min(PAST_LEN, POOL_HIST - 1)
    y_shape = (tp + ts, tok_w + MEM_W)

    x = (x_prompt.reshape(tp, d), x_sample.reshape(ts, d))

    mem_rows = mem_prompt.reshape(bp * n_mem, d)
    p_kv, p_mem_k, p_mem_v = mem_kv(mem_rows, norm_mem, w_mem_kv)
    p_kv = p_kv.reshape(depth, bp, n_mem, 2 * MEM_W)
    p_mem_k = p_mem_k.reshape(depth, bp, n_mem, N_MEM_HEADS, HEAD_DIM)
    p_mem_v = p_mem_v.reshape(depth, bp, n_mem, N_MEM_HEADS, HEAD_DIM)
    cache_k = cache_mem_k.reshape(depth, bs, n_mem * N_MEM_HEADS, HEAD_DIM)
    cache_v = cache_mem_v.reshape(depth, bs, n_mem * N_MEM_HEADS, HEAD_DIM)

    w_in_delta_t = jnp.swapaxes(w_in_delta, 1, 2)
    zero_s = jnp.zeros((n_delta, bp, n_heads, HEAD_DIM, HEAD_DIM), F32)
    zero_conv = jnp.zeros((1, bp, CONV_W - 1, 3 * tok_w), F32)
    zero_pool = jnp.zeros((1, bp, POOL_HIST - 1, tok_w), F32)
    conv_in = jnp.swapaxes(state_delta_conv, 1, 2)
    pool_in = jnp.swapaxes(state_pool, 1, 2)

    p_conv, p_pool = [], []
    p_s = s_s = s_conv = s_pool = None
    di = pi = 0
    for l in range(depth):
        h = rmsnorm_concat(*x, norm_mix[l], BF16) if l == 0 else rmsnorm(x, norm_mix[l], BF16)
        if l % 2 == 0:
            proj = matmul(h, w_in_delta_t, di, ncols=main_w, tn=1024, out_dtype=F32, w_transposed=True)
            tail0 = main_w // LANE
            small = matmul(h, w_in_delta_t, di, ncols=MEM_W + LANE, tn=LANE, out_dtype=F32, w_transposed=True,
                           col_blocks=tuple(range(tail0, tail0 + (MEM_W + LANE) // LANE)),
                           out_slices=((2 * n_heads, 2 * n_heads + MEM_W), (0, LANE)))
            gates = (conv_w[di], a_log[di], dt_bias[di], delta_onorm[di])
            y, conv_p, p_s = delta_mixer(proj, small, 0, zero_conv, 0, None, zero_s, di, p_s, None, y_shape, *gates,
                                         batch=bp, seq=lp, chunk=64, bb=1, n_sub=4, seq_group=1, seq_minor=False)
            y, s_conv, s_s = delta_mixer(proj, small, tp, conv_in, di, s_conv, state_delta_S, di, s_s, y, y_shape,
                                         *gates, batch=bs, seq=ls, chunk=64, bb=8, n_sub=1, seq_group=4,
                                         seq_minor=True)
            p_conv.append(conv_p)
            q_src, q_blk = small, 0
            di += 1
        else:
            proj = matmul(h, w_in_pool, pi, ncols=tok_w + MEM_W, tn=1024, out_dtype=F32)
            y, pool_p = pool_mixer(proj, 0, zero_pool, 0, None, w_pool_grp, pool_scale, pi, None, y_shape,
                                   batch=bp, seq=lp, chunk=256, bb=1, n_past=0, seq_minor=False)
            y, s_pool = pool_mixer(proj, tp, pool_in, pi, s_pool, w_pool_grp, pool_scale, pi, y, y_shape,
                                   batch=bs, seq=ls, chunk=128, bb=16, n_past=n_past, seq_minor=True)
            p_pool.append(pool_p)
            q_src, q_blk = proj, tok_w // MEM_W
            pi += 1

        y = cross_attn_prompt(q_src, q_blk, p_kv, l, y, batch=bp, seq=lp, tq=1024)
        y = cross_attn_sample(q_src, q_blk, tp, cache_k, cache_v, l, y, batch=bs, seq=ls, bb=16)

        row_tiles = (256, 128) if l == 0 else (544, 512, 256, 128)
        fits = (lambda c: tp % c == 0 and ts % c == 0) if l == 0 else (lambda c: (tp + ts) % c == 0)
        x, h2 = matmul_residual_norm(y, w_out, l, x, norm_ffn[l], tm=next(c for c in row_tiles if fits(c)))
        act = matmul(h2, w_gate_up, l, ncols=d_ff, tn=512, out_dtype=BF16, col_blocks=(0, d_ff // 512), swiglu=True,
                     tm_max=2176)
        x = matmul(act, w_down, l, ncols=d, tn=512, out_dtype=F32, residual=x, tm_max=512)

    y_p, y_s = rmsnorm_split(x, norm_final, tp)
    return (y_p.reshape(bp, lp, d), y_s.reshape(bs, ls, d),
            p_s, jnp.stack(p_conv), jnp.stack(p_pool), p_mem_k, p_mem_v,
            s_s, jnp.swapaxes(s_conv, 1, 2), jnp.swapaxes(s_pool, 1, 2))
```
